```python
import math
import jax
import jax.numpy as jnp
from jax import lax
import numpy as np

D_MODEL = 1024
BATCH = 16
SEQ = 256
DEPTH = 4
DEC_BATCH = 4
DEC_SEQ = 4096
PAST_LEN = 256

GRID_W = 64
POOL_WINDOWS = (2, 4, 8, 16)
POOL_GROUP = 128
POOL_WIDTH = POOL_GROUP * len(POOL_WINDOWS)
N_HEADS = 8
QK_NOPE = 64
QK_ROPE = 32
QK_HEAD = QK_NOPE + QK_ROPE
V_HEAD = 64
Q_RANK = 384
KV_RANK = 256
ATTN_WIDTH = N_HEADS * V_HEAD
MIX_WIDTH = POOL_WIDTH + ATTN_WIDTH
IN_WIDTH = POOL_WIDTH + Q_RANK + KV_RANK + QK_ROPE
N_EXPERTS = 16
N_GROUPS = 4
EXPERTS_PER_GROUP = N_EXPERTS // N_GROUPS
TOP_K = 2
D_EXPERT = 256
ROPE_BASE = 10000.0
EPS = 1e-6
Q_BLOCK = 128

kernel_name = 'hybrid_pool_mla_moe_diffusion_step'


def rms_norm(x, g):
    xf = x.astype(jnp.float32)
    y = xf * lax.rsqrt(jnp.mean(xf * xf, axis=-1, keepdims=True) + EPS)
    return (y * g.astype(jnp.float32)).astype(x.dtype)


def axial_rope(rows):
    row = jnp.repeat(jnp.arange(rows), GRID_W).astype(jnp.float32)
    col = jnp.tile(jnp.arange(GRID_W), rows).astype(jnp.float32)
    n_freq = QK_ROPE // 4
    inv = ROPE_BASE ** (-jnp.arange(n_freq, dtype=jnp.float32) / n_freq)
    ang = jnp.concatenate([row[:, None] * inv, col[:, None] * inv], axis=-1)
    return jnp.cos(ang)[:, None, :], jnp.sin(ang)[:, None, :]


def apply_rope(t, cos, sin):
    half = QK_ROPE // 2
    nope = t[..., :QK_NOPE]
    r1 = t[..., QK_NOPE:QK_NOPE + half].astype(jnp.float32)
    r2 = t[..., QK_NOPE + half:].astype(jnp.float32)
    rot = jnp.concatenate([r1 * cos - r2 * sin, r1 * sin + r2 * cos], axis=-1).astype(t.dtype)
    return jnp.concatenate([nope, rot], axis=-1)


def pool_mixer(u, pool_w, pool_scale):
    B, L, _ = u.shape
    uf = u.astype(jnp.float32)
    csum = jnp.concatenate([jnp.zeros((B, 1, POOL_WIDTH), jnp.float32), jnp.cumsum(uf, axis=1)], axis=1)
    t = jnp.arange(L)
    outs = []
    for g, w in enumerate(POOL_WINDOWS):
        lo = jnp.clip(t - w // 2, 0, L)
        hi = jnp.clip(t + w // 2, 0, L)
        sl = slice(g * POOL_GROUP, (g + 1) * POOL_GROUP)
        cg = csum[:, :, sl]
        mean = (cg[:, hi] - cg[:, lo]) / (hi - lo).astype(jnp.float32)[None, :, None]
        d = (mean - uf[:, :, sl]).astype(u.dtype)
        outs.append(d @ pool_w[g])
    return jnp.concatenate(outs, axis=-1) * pool_scale


def mla_keys(ckv, kr, w_ukv, k_norm_g):
    B, N, _ = ckv.shape
    kv = (ckv @ w_ukv).reshape(B, N, N_HEADS, QK_NOPE + V_HEAD)
    k_nope, v = kv[..., :QK_NOPE], kv[..., QK_NOPE:]
    k = jnp.concatenate([k_nope, jnp.broadcast_to(kr[:, :, None, :], (B, N, N_HEADS, QK_ROPE))], axis=-1)
    return rms_norm(k, k_norm_g), v


def attend(q, k, v):
    B, L, H, Dk = q.shape
    nb = L // Q_BLOCK
    qb = jnp.moveaxis(q.reshape(B, nb, Q_BLOCK, H, Dk), 1, 0)
    scale = 1.0 / math.sqrt(Dk)

    def block(qi):
        s = jnp.einsum('bqhd,bkhd->bhqk', qi, k, preferred_element_type=jnp.float32) * scale
        p = jax.nn.softmax(s, axis=-1).astype(v.dtype)
        return jnp.einsum('bhqk,bkhd->bqhd', p, v)

    o = lax.map(block, qb)
    return jnp.moveaxis(o, 0, 1).reshape(B, L, H * v.shape[-1])


def moe(h, router_w, router_b, w_gate, w_up, w_down):
    B, L, D = h.shape
    t = h.reshape(B * L, D)
    scores = jax.nn.sigmoid(jnp.dot(t, router_w, preferred_element_type=jnp.float32))
    sel = scores + router_b.astype(jnp.float32)
    grp_score = jnp.sum(lax.top_k(sel.reshape(-1, N_GROUPS, EXPERTS_PER_GROUP), TOP_K)[0], axis=-1)
    best = jnp.argmax(grp_score, axis=-1)
    in_group = (jnp.arange(N_EXPERTS) // EXPERTS_PER_GROUP)[None, :] == best[:, None]
    _, idx = lax.top_k(jnp.where(in_group, sel, -jnp.inf), TOP_K)
    wts = jnp.take_along_axis(scores, idx, axis=-1)
    wts = wts / jnp.sum(wts, axis=-1, keepdims=True)
    gates = jnp.sum(jax.nn.one_hot(idx, N_EXPERTS, dtype=jnp.float32) * wts[..., None], axis=1).astype(h.dtype)
    a = jnp.einsum('td,edf->tef', t, w_gate)
    b = jnp.einsum('td,edf->tef', t, w_up)
    act = jax.nn.silu(a) * b * gates[:, :, None]
    return jnp.einsum('tef,efd->td', act, w_down).reshape(B, L, D)


def trunk_layer(x, cond, p, router_w, router_b, latent_ctx):
    B, L, _ = x.shape
    mod = (jax.nn.silu(cond) @ p['ada_w'] + p['ada_b'])[:, None, :]
    shift1, scale1, gate1, shift2, scale2, gate2 = jnp.split(mod, 6, axis=-1)
    h = rms_norm(x, p['norm1_g']) * (1 + scale1) + shift1
    proj = h @ p['w_in']
    u, cq, ckv, kr = jnp.split(proj, [POOL_WIDTH, POOL_WIDTH + Q_RANK, POOL_WIDTH + Q_RANK + KV_RANK], axis=-1)
    pool_out = pool_mixer(u, p['pool_w'], p['pool_scale'])
    ckv = rms_norm(ckv, p['kv_a_norm_g'])
    q = (rms_norm(cq, p['q_a_norm_g']) @ p['w_uq']).reshape(B, L, N_HEADS, QK_HEAD)
    q = rms_norm(q, p['q_norm_g'])
    k, v = mla_keys(ckv, kr, p['w_ukv'], p['k_norm_g'])
    if latent_ctx is not None:
        k_ctx, v_ctx, cos, sin = latent_ctx
        q = apply_rope(q, cos, sin)
        k = jnp.concatenate([k_ctx, apply_rope(k, cos, sin)], axis=1)
        v = jnp.concatenate([v_ctx, v], axis=1)
    attn = attend(q, k, v)
    x = x + gate1 * (jnp.concatenate([pool_out, attn], axis=-1) @ p['w_out'])
    h2 = rms_norm(x, p['norm2_g']) * (1 + scale2) + shift2
    x = x + gate2 * moe(h2, router_w, router_b, p['w_gate'], p['w_up'], p['w_down'])
    return x, ckv, kr


def setup_inputs(seed: int = 0) -> dict:
    key = jax.random.key(seed)
    ks = jax.random.split(key, 32)

    def nrm(k, shape, s):
        return jax.random.normal(k, shape, jnp.float32) * s

    def gain(k, shape):
        return 1.0 + 0.05 * jax.random.normal(k, shape, jnp.float32)

    return {
        'x_prompt': nrm(ks[0], (BATCH, SEQ, D_MODEL), 1.0),
        'x_sample': nrm(ks[1], (DEC_BATCH, DEC_SEQ, D_MODEL), 1.0),
        'cache_ckv': nrm(ks[2], (DEC_BATCH, DEPTH, PAST_LEN, KV_RANK), 1.0),
        'cache_krope': nrm(ks[3], (DEC_BATCH, DEPTH, PAST_LEN, QK_ROPE), 1.0),
        'c': nrm(ks[4], (DEC_BATCH, D_MODEL), 1.0),
        'c_ctx': nrm(ks[5], (D_MODEL,), 1.0),
        'ada_w': nrm(ks[6], (DEPTH, D_MODEL, 6 * D_MODEL), 0.5 * D_MODEL ** -0.5),
        'ada_b': nrm(ks[7], (DEPTH, 6 * D_MODEL), 0.02),
        'norm1_g': gain(ks[8], (DEPTH, D_MODEL)),
        'norm2_g': gain(ks[9], (DEPTH, D_MODEL)),
        'w_in': nrm(ks[10], (DEPTH, D_MODEL, IN_WIDTH), D_MODEL ** -0.5),
        'pool_w': nrm(ks[11], (DEPTH, len(POOL_WINDOWS), POOL_GROUP, POOL_GROUP), POOL_GROUP ** -0.5),
        'pool_scale': gain(ks[12], (DEPTH, POOL_WIDTH)),
        'q_a_norm_g': gain(ks[13], (DEPTH, Q_RANK)),
        'w_uq': nrm(ks[14], (DEPTH, Q_RANK, N_HEADS * QK_HEAD), Q_RANK ** -0.5),
        'kv_a_norm_g': gain(ks[15], (DEPTH, KV_RANK)),
        'w_ukv': nrm(ks[16], (DEPTH, KV_RANK, N_HEADS * (QK_NOPE + V_HEAD)), KV_RANK ** -0.5),
        'q_norm_g': gain(ks[17], (DEPTH, QK_HEAD)),
        'k_norm_g': gain(ks[18], (DEPTH, QK_HEAD)),
        'w_out': nrm(ks[19], (DEPTH, MIX_WIDTH, D_MODEL), MIX_WIDTH ** -0.5),
        'router_w': nrm(ks[20], (D_MODEL, N_EXPERTS), D_MODEL ** -0.5),
        'router_b': nrm(ks[21], (N_EXPERTS,), 0.01),
        'w_gate': nrm(ks[22], (DEPTH, N_EXPERTS, D_MODEL, D_EXPERT), D_MODEL ** -0.5),
        'w_up': nrm(ks[23], (DEPTH, N_EXPERTS, D_MODEL, D_EXPERT), D_MODEL ** -0.5),
        'w_down': nrm(ks[24], (DEPTH, N_EXPERTS, D_EXPERT, D_MODEL), D_EXPERT ** -0.5),
    }


def reference(x_prompt, x_sample, cache_ckv, cache_krope, c, c_ctx, ada_w, ada_b, norm1_g, norm2_g, w_in, pool_w,
              pool_scale, q_a_norm_g, w_uq, kv_a_norm_g, w_ukv, q_norm_g, k_norm_g, w_out, router_w, router_b,
              w_gate, w_up, w_down):
    rows = x_sample.shape[1] // GRID_W
    cos, sin = axial_rope(rows)
    y_prompt, y_sample = x_prompt, x_sample
    ckv_list, kr_list = [], []
    for l in range(DEPTH):
        p = dict(ada_w=ada_w[l], ada_b=ada_b[l], norm1_g=norm1_g[l], norm2_g=norm2_g[l], w_in=w_in[l],
                 pool_w=pool_w[l], pool_scale=pool_scale[l], q_a_norm_g=q_a_norm_g[l], w_uq=w_uq[l],
                 kv_a_norm_g=kv_a_norm_g[l], w_ukv=w_ukv[l], q_norm_g=q_norm_g[l], k_norm_g=k_norm_g[l],
                 w_out=w_out[l], w_gate=w_gate[l], w_up=w_up[l], w_down=w_down[l])
        y_prompt, ckv, kr = trunk_layer(y_prompt, c_ctx[None, :], p, router_w, router_b, None)
        ckv_list.append(ckv)
        kr_list.append(kr)
        k_ctx, v_ctx = mla_keys(cache_ckv[:, l], cache_krope[:, l], w_ukv[l], k_norm_g[l])
        y_sample, _, _ = trunk_layer(y_sample, c, p, router_w, router_b, (k_ctx, v_ctx, cos, sin))
    new_ckv = jnp.stack(ckv_list, axis=1)
    new_krope = jnp.stack(kr_list, axis=1)
    return (y_prompt, y_sample, new_ckv, new_krope)
```

```python
import functools
import math

import jax
import jax.numpy as jnp
from jax import lax
from jax.experimental import pallas as pl
from jax.experimental.pallas import tpu as pltpu

D_MODEL = 1024
DEPTH = 4
GRID_W = 64
POOL_WINDOWS = (2, 4, 8, 16)
POOL_GROUP = 128
POOL_WIDTH = POOL_GROUP * len(POOL_WINDOWS)
N_HEADS = 8
QK_NOPE = 64
QK_ROPE = 32
QK_HEAD = QK_NOPE + QK_ROPE
V_HEAD = 64
Q_RANK = 384
KV_RANK = 256
ATTN_WIDTH = N_HEADS * V_HEAD
N_EXPERTS = 16
N_GROUPS = 4
EXPERTS_PER_GROUP = N_EXPERTS // N_GROUPS
D_EXPERT = 256
ROPE_BASE = 10000.0
EPS = 1e-6

LANE = 128
HEAD_PAD = N_HEADS * LANE
IN_PAD = POOL_WIDTH + Q_RANK + KV_RANK + LANE
POOL_HALO = max(POOL_WINDOWS)
TOKEN_BLOCK = 256
MOE_BLOCK = 1024
EXPERT_CHUNK = 4
VMEM_LIMIT = 56 * 1024 * 1024

BF16 = jnp.bfloat16
F32 = jnp.float32


def _dot(a, b):
    return jnp.dot(a, b, preferred_element_type=F32)


def _dot_nt(a, b):
    return lax.dot_general(a, b, (((1,), (1,)), ((), ())), preferred_element_type=F32)


def _rms(x, g):
    return x * lax.rsqrt(jnp.mean(x * x, axis=-1, keepdims=True) + EPS) * g


def _sigmoid(x):
    return 1.0 / (1.0 + jnp.exp(-x))


def _params(n_axes):
    return pltpu.CompilerParams(dimension_semantics=("arbitrary",) * n_axes, vmem_limit_bytes=VMEM_LIMIT)


def _mod_kernel(cond_ref, w_ref, b_ref, o_ref):
    cond = cond_ref[...]
    s = (cond * _sigmoid(cond)).astype(BF16)
    o_ref[...] = _dot(s, w_ref[...].astype(BF16)) + b_ref[...]


def _mod_table(conds, ada_w, ada_b):
    n_rows = conds.shape[0]
    tn = 1536
    return pl.pallas_call(
        _mod_kernel,
        grid=(DEPTH, 6 * D_MODEL // tn),
        in_specs=[
            pl.BlockSpec((n_rows, D_MODEL), lambda l, n: (0, 0)),
            pl.BlockSpec((None, D_MODEL, tn), lambda l, n: (l, 0, n)),
            pl.BlockSpec((None, 1, tn), lambda l, n: (l, 0, n)),
        ],
        out_specs=pl.BlockSpec((None, n_rows, tn), lambda l, n: (l, 0, n)),
        out_shape=jax.ShapeDtypeStruct((DEPTH, n_rows, 6 * D_MODEL), F32),
        compiler_params=_params(2),
        name="mod_table",
    )(conds, ada_w, ada_b.reshape(DEPTH, 1, 6 * D_MODEL))


def _head_norm(t, g):
    ms = jnp.sum(t * t, axis=-1, keepdims=True) * (1.0 / QK_HEAD)
    return t * lax.rsqrt(ms + EPS) * g


def _rope(t, cos, sin_lo, sin_hi):
    return t * cos + pltpu.roll(t, LANE - QK_ROPE // 2, 1) * sin_lo + pltpu.roll(t, QK_ROPE // 2, 1) * sin_hi


def _cache_keys_kernel(ckv_ref, kr_ref, wuk_ref, wuv_ref, gk_ref, k_ref, v_ref):
    ckv = ckv_ref[...].astype(BF16)
    kn = _dot(ckv, wuk_ref[...])
    v_ref[...] = _dot(ckv, wuv_ref[...]).astype(BF16)
    kr = kr_ref[...]
    for h in range(N_HEADS):
        sl = slice(h * LANE, (h + 1) * LANE)
        k_ref[:, sl] = _head_norm(kn[:, sl] + kr, gk_ref[...]).astype(BF16)


def _cache_keys(cache_ckv, cache_kr_pad, wuk, wuv, gk):
    nb, _, n, _ = cache_ckv.shape
    return pl.pallas_call(
        _cache_keys_kernel,
        grid=(DEPTH, nb),
        in_specs=[
            pl.BlockSpec((None, None, n, KV_RANK), lambda l, b: (b, l, 0, 0)),
            pl.BlockSpec((None, None, n, LANE), lambda l, b: (b, l, 0, 0)),
            pl.BlockSpec((None, KV_RANK, HEAD_PAD), lambda l, b: (l, 0, 0)),
            pl.BlockSpec((None, KV_RANK, HEAD_PAD), lambda l, b: (l, 0, 0)),
            pl.BlockSpec((None, 1, LANE), lambda l, b: (l, 0, 0)),
        ],
        out_specs=[
            pl.BlockSpec((None, None, n, HEAD_PAD), lambda l, b: (l, b, 0, 0)),
            pl.BlockSpec((None, None, n, HEAD_PAD), lambda l, b: (l, b, 0, 0)),
        ],
        out_shape=[jax.ShapeDtypeStruct((DEPTH, nb, n, HEAD_PAD), BF16)] * 2,
        compiler_params=_params(2),
        name="cache_keys",
    )(cache_ckv, cache_kr_pad, wuk, wuv, gk)


def _inproj_kernel(*refs, latent):
    if latent:
        (x_ref, mod_ref, g1_ref, win_ref, qag_ref, wuq_ref, kvg_ref, wuk_ref, wuv_ref, gq_ref, gk_ref, rope_ref,
         u_ref, q_ref, k_ref, v_ref) = refs
    else:
        (x_ref, mod_ref, g1_ref, win_ref, qag_ref, wuq_ref, kvg_ref, wuk_ref, wuv_ref, gq_ref, gk_ref,
         u_ref, q_ref, k_ref, v_ref, ckv_ref, kr_ref) = refs
    shift1 = mod_ref[0:1, :]
    scale1 = mod_ref[1:2, :]
    h = _rms(x_ref[...], g1_ref[...]) * (1.0 + scale1) + shift1
    proj = _dot(h.astype(BF16), win_ref[...])
    u_ref[...] = proj[:, :POOL_WIDTH]
    cq = proj[:, POOL_WIDTH:POOL_WIDTH + Q_RANK]
    q = _dot(_rms(cq, qag_ref[...]).astype(BF16), wuq_ref[...])
    ckv = _rms(proj[:, POOL_WIDTH + Q_RANK:POOL_WIDTH + Q_RANK + KV_RANK], kvg_ref[...])
    kr = proj[:, POOL_WIDTH + Q_RANK + KV_RANK:]
    if not latent:
        ckv_ref[...] = ckv
        kr_ref[...] = kr
    ckv_b = ckv.astype(BF16)
    kn = _dot(ckv_b, wuk_ref[...])
    v_ref[...] = _dot(ckv_b, wuv_ref[...]).astype(BF16)
    if latent:
        cos = rope_ref[:, 0:LANE]
        sin_lo = rope_ref[:, LANE:2 * LANE]
        sin_hi = rope_ref[:, 2 * LANE:3 * LANE]
    for hd in range(N_HEADS):
        sl = slice(hd * LANE, (hd + 1) * LANE)
        qh = _head_norm(q[:, sl], gq_ref[...])
        kh = _head_norm(kn[:, sl] + kr, gk_ref[...])
        if latent:
            qh = _rope(qh, cos, sin_lo, sin_hi)
            kh = _rope(kh, cos, sin_lo, sin_hi)
        q_ref[:, sl] = qh.astype(BF16)
        k_ref[:, sl] = kh.astype(BF16)


def _inproj(x, mod_l, w, l, *, latent, seq_len, rope=None):
    t = x.shape[0]
    tb = TOKEN_BLOCK
    bps = seq_len // tb
    if latent:
        cond_row = lambda j: (1 + j // bps, 0, 0)
    else:
        cond_row = lambda j: (0, 0, 0)
    wspec = lambda shape: pl.BlockSpec((None,) + shape, lambda j: (l,) + (0,) * len(shape))
    in_specs = [
        pl.BlockSpec((tb, D_MODEL), lambda j: (j, 0)),
        pl.BlockSpec((None, 6, D_MODEL), cond_row),
        wspec((1, D_MODEL)),
        wspec((D_MODEL, IN_PAD)),
        wspec((1, Q_RANK)),
        wspec((Q_RANK, HEAD_PAD)),
        wspec((1, KV_RANK)),
        wspec((KV_RANK, HEAD_PAD)),
        wspec((KV_RANK, HEAD_PAD)),
        wspec((1, LANE)),
        wspec((1, LANE)),
    ]
    args = [x, mod_l, w['g1'], w['w_in'], w['qag'], w['w_uq'], w['kvg'], w['w_uk'], w['w_uv'], w['gq'], w['gk']]
    tok = lambda width: pl.BlockSpec((tb, width), lambda j: (j, 0))
    out_specs = [tok(POOL_WIDTH), tok(HEAD_PAD), tok(HEAD_PAD), tok(HEAD_PAD)]
    out_shape = [jax.ShapeDtypeStruct((t, POOL_WIDTH), F32)] + [jax.ShapeDtypeStruct((t, HEAD_PAD), BF16)] * 3
    if latent:
        in_specs.append(pl.BlockSpec((tb, 3 * LANE), lambda j: (j % bps, 0)))
        args.append(rope)
    else:
        out_specs += [tok(KV_RANK), tok(LANE)]
        out_shape += [jax.ShapeDtypeStruct((t, KV_RANK), F32), jax.ShapeDtypeStruct((t, LANE), F32)]
    return pl.pallas_call(
        functools.partial(_inproj_kernel, latent=latent),
        grid=(t // tb,),
        in_specs=in_specs,
        out_specs=out_specs,
        out_shape=out_shape,
        compiler_params=_params(1),
        name="inproj_latent" if latent else "inproj_context",
    )(*args)


def _attn_kernel(*refs, n_seg):
    q_ref = refs[0]
    kv_refs = refs[1:1 + 2 * n_seg]
    o_ref = refs[1 + 2 * n_seg]
    outs = []
    for hd in range(N_HEADS):
        sl = slice(hd * LANE, (hd + 1) * LANE)
        qh = q_ref[:, sl]
        scores = [_dot_nt(qh, kv_refs[2 * s][:, sl]) for s in range(n_seg)]
        m = functools.reduce(jnp.maximum, [jnp.max(s, axis=-1, keepdims=True) for s in scores])
        probs = [jnp.exp(s - m) for s in scores]
        denom = functools.reduce(jnp.add, [jnp.sum(p, axis=-1, keepdims=True) for p in probs])
        o = functools.reduce(jnp.add, [_dot(p.astype(BF16), kv_refs[2 * s + 1][:, sl]) for s, p in enumerate(probs)])
        outs.append((o * (1.0 / denom))[:, :V_HEAD])
    o_ref[...] = jnp.concatenate(outs, axis=-1).astype(BF16)


def _attention(q, kvs, *, seq_len):
    t = q.shape[0]
    tb = TOKEN_BLOCK
    bps = seq_len // tb
    in_specs = [pl.BlockSpec((tb, HEAD_PAD), lambda j: (j, 0))]
    args = [q]
    for k, v in kvs:
        n = k.shape[1]
        in_specs += [pl.BlockSpec((None, n, HEAD_PAD), lambda j: (j // bps, 0, 0))] * 2
        args += [k, v]
    return pl.pallas_call(
        functools.partial(_attn_kernel, n_seg=len(kvs)),
        grid=(t // tb,),
        in_specs=in_specs,
        out_specs=pl.BlockSpec((tb, ATTN_WIDTH), lambda j: (j, 0)),
        out_shape=jax.ShapeDtypeStruct((t, ATTN_WIDTH), BF16),
        compiler_params=_params(1),
        name="attention_%dseg" % len(kvs),
    )(*args)


def _route(logits_t, bias):
    scores = _sigmoid(logits_t)
    sel = scores + bias
    s_rows = [scores[e:e + 1, :] for e in range(N_EXPERTS)]
    r = [sel[e:e + 1, :] for e in range(N_EXPERTS)]
    picked = []
    group_score = []
    for g in range(N_GROUPS):
        members = range(g * EXPERTS_PER_GROUP, (g + 1) * EXPERTS_PER_GROUP)
        total = None
        for e in members:
            rank = None
            for j in members:
                if j == e:
                    continue
                ahead = (r[j] >= r[e]) if j < e else (r[j] > r[e])
                ahead = ahead.astype(F32)
                rank = ahead if rank is None else rank + ahead
            pick = rank < 2.0
            picked.append(pick)
            contrib = jnp.where(pick, r[e], 0.0)
            total = contrib if total is None else total + contrib
        group_score.append(total)
    gates = []
    for g in range(N_GROUPS):
        best = None
        for j in range(N_GROUPS):
            if j == g:
                continue
            wins = (group_score[g] > group_score[j]) if j < g else (group_score[g] >= group_score[j])
            best = wins if best is None else jnp.logical_and(best, wins)
        for e in range(g * EXPERTS_PER_GROUP, (g + 1) * EXPERTS_PER_GROUP):
            gates.append(jnp.where(jnp.logical_and(best, picked[e]), s_rows[e], 0.0))
    denom = functools.reduce(jnp.add, gates)
    inv = 1.0 / denom
    return [g * inv for g in gates]


def _post_kernel(x_ref, up_ref, uc_ref, un_ref, attn_ref, mod_ref, poolw_ref, pscale_ref, wout_ref, g2_ref, rwt_ref,
                 rb_ref, x1_ref, h2_ref, gates_ref, uext_ref, *, blocks_per_seq, seq_len):
    tb = TOKEN_BLOCK
    halo = POOL_HALO
    jb = pl.program_id(0) % blocks_per_seq
    uext_ref[0:halo, :] = jnp.where(jb == 0, 0.0, up_ref[tb - halo:tb, :])
    uext_ref[halo:halo + tb, :] = uc_ref[...]
    uext_ref[halo + tb:, :] = jnp.where(jb == blocks_per_seq - 1, 0.0, un_ref[0:halo, :])
    pos = jb * tb + lax.broadcasted_iota(jnp.int32, (tb, 1), 0)
    pooled = []
    for g, w in enumerate(POOL_WINDOWS):
        half = w // 2
        cols = slice(g * POOL_GROUP, (g + 1) * POOL_GROUP)
        total = None
        for d in range(-half, half):
            part = uext_ref[halo + d:halo + d + tb, cols]
            total = part if total is None else total + part
        count = jnp.minimum(pos + half, seq_len) - jnp.maximum(pos - half, 0)
        diff = total / count.astype(F32) - uc_ref[:, cols]
        pooled.append(_dot(diff.astype(BF16), poolw_ref[g]))
    pool = jnp.concatenate(pooled, axis=-1) * pscale_ref[...]
    mixed = _dot(jnp.concatenate([pool.astype(BF16), attn_ref[...]], axis=-1), wout_ref[...])
    gate1 = mod_ref[2:3, :]
    shift2 = mod_ref[3:4, :]
    scale2 = mod_ref[4:5, :]
    x1 = x_ref[...] + gate1 * mixed
    x1_ref[...] = x1
    h2 = (_rms(x1, g2_ref[...]) * (1.0 + scale2) + shift2).astype(BF16)
    h2_ref[...] = h2
    gate_rows = _route(_dot_nt(rwt_ref[...], h2), rb_ref[...])
    zeros = jnp.zeros((LANE - EXPERT_CHUNK, tb), F32)
    pieces = []
    for c in range(N_EXPERTS // EXPERT_CHUNK):
        pieces += gate_rows[c * EXPERT_CHUNK:(c + 1) * EXPERT_CHUNK] + [zeros]
    gates_ref[...] = jnp.concatenate(pieces, axis=0).T


def _post(x, u, attn, mod_l, w, router_wt, router_b, l, *, latent, seq_len):
    t = x.shape[0]
    tb = TOKEN_BLOCK
    bps = seq_len // tb
    n_chunks = N_EXPERTS // EXPERT_CHUNK
    if latent:
        cond_row = lambda j: (1 + j // bps, 0, 0)
    else:
        cond_row = lambda j: (0, 0, 0)
    wspec = lambda shape: pl.BlockSpec((None,) + shape, lambda j: (l,) + (0,) * len(shape))
    tok = lambda width: pl.BlockSpec((tb, width), lambda j: (j, 0))
    n_blocks = t // tb
    in_specs = [
        tok(D_MODEL),
        pl.BlockSpec((tb, POOL_WIDTH), lambda j: (jnp.maximum(j - 1, 0), 0)),
        tok(POOL_WIDTH),
        pl.BlockSpec((tb, POOL_WIDTH), lambda j: (jnp.minimum(j + 1, n_blocks - 1), 0)),
        tok(ATTN_WIDTH),
        pl.BlockSpec((None, 6, D_MODEL), cond_row),
        wspec((len(POOL_WINDOWS), POOL_GROUP, POOL_GROUP)),
        wspec((1, POOL_WIDTH)),
        wspec((2 * ATTN_WIDTH, D_MODEL)),
        wspec((1, D_MODEL)),
        pl.BlockSpec((N_EXPERTS, D_MODEL), lambda j: (0, 0)),
        pl.BlockSpec((N_EXPERTS, 1), lambda j: (0, 0)),
    ]
    return pl.pallas_call(
        functools.partial(_post_kernel, blocks_per_seq=bps, seq_len=seq_len),
        grid=(n_blocks,),
        in_specs=in_specs,
        out_specs=[tok(D_MODEL), tok(D_MODEL), tok(n_chunks * LANE)],
        out_shape=[
            jax.ShapeDtypeStruct((t, D_MODEL), F32),
            jax.ShapeDtypeStruct((t, D_MODEL), BF16),
            jax.ShapeDtypeStruct((t, n_chunks * LANE), F32),
        ],
        scratch_shapes=[pltpu.VMEM((tb + 2 * POOL_HALO, POOL_WIDTH), F32)],
        compiler_params=_params(1),
        name="post_latent" if latent else "post_context",
    )(x, u, u, u, attn, mod_l, w['pool_w'], w['pool_scale'], w['w_out'], w['g2'], router_wt, router_b)


def _moe_kernel(h2_ref, gates_ref, x1_ref, mod_ref, wg_ref, wu_ref, wd_ref, o_ref, acc_ref):
    c = pl.program_id(1)

    @pl.when(c == 0)
    def _():
        acc_ref[...] = jnp.zeros_like(acc_ref)

    h = h2_ref[...]
    total = None
    for e in range(EXPERT_CHUNK):
        a = _dot(h, wg_ref[e])
        b = _dot(h, wu_ref[e])
        act = a * _sigmoid(a) * b * gates_ref[:, e:e + 1]
        y = _dot(act.astype(BF16), wd_ref[e])
        total = y if total is None else total + y
    acc_ref[...] += total

    @pl.when(c == pl.num_programs(1) - 1)
    def _():
        o_ref[...] = x1_ref[...] + mod_ref[5:6, :] * acc_ref[...]


def _moe(h2, gates, x1, mod_l, w, l, *, latent, seq_len):
    t = h2.shape[0]
    tm = MOE_BLOCK
    bps = seq_len // tm if latent else 1
    if latent:
        cond_row = lambda i, c: (1 + i // bps, 0, 0)
    else:
        cond_row = lambda i, c: (0, 0, 0)
    n_chunks = N_EXPERTS // EXPERT_CHUNK
    return pl.pallas_call(
        _moe_kernel,
        grid=(t // tm, n_chunks),
        in_specs=[
            pl.BlockSpec((tm, D_MODEL), lambda i, c: (i, 0)),
            pl.BlockSpec((tm, LANE), lambda i, c: (i, c)),
            pl.BlockSpec((tm, D_MODEL), lambda i, c: (i, 0)),
            pl.BlockSpec((None, 6, D_MODEL), cond_row),
            pl.BlockSpec((None, EXPERT_CHUNK, D_MODEL, D_EXPERT), lambda i, c: (l, c, 0, 0)),
            pl.BlockSpec((None, EXPERT_CHUNK, D_MODEL, D_EXPERT), lambda i, c: (l, c, 0, 0)),
            pl.BlockSpec((None, EXPERT_CHUNK, D_EXPERT, D_MODEL), lambda i, c: (l, c, 0, 0)),
        ],
        out_specs=pl.BlockSpec((tm, D_MODEL), lambda i, c: (i, 0)),
        out_shape=jax.ShapeDtypeStruct((t, D_MODEL), F32),
        scratch_shapes=[pltpu.VMEM((tm, D_MODEL), F32)],
        compiler_params=_params(2),
        name="experts_latent" if latent else "experts_context",
    )(h2, gates, x1, mod_l, w['w_gate'], w['w_up'], w['w_down'])


def _pad_heads(w, width):
    lead = w.shape[:-1]
    w = w.reshape(lead + (N_HEADS, width))
    w = jnp.pad(w, [(0, 0)] * len(lead) + [(0, 0), (0, LANE - width)])
    return w.reshape(lead + (HEAD_PAD,))


def _pad_gain(g, scale=1.0):
    return jnp.pad(g * scale, ((0, 0), (0, LANE - QK_HEAD)))[:, None, :]


def _rope_table(rows):
    row = jnp.repeat(jnp.arange(rows), GRID_W).astype(F32)
    col = jnp.tile(jnp.arange(GRID_W), rows).astype(F32)
    n_freq = QK_ROPE // 4
    inv = ROPE_BASE ** (-jnp.arange(n_freq, dtype=F32) / n_freq)
    ang = jnp.concatenate([row[:, None] * inv, col[:, None] * inv], axis=-1)
    cos, sin = jnp.cos(ang), jnp.sin(ang)
    n = ang.shape[0]
    half = QK_ROPE // 2
    pad_hi = LANE - QK_HEAD
    cos_t = jnp.concatenate([jnp.ones((n, QK_NOPE), F32), cos, cos, jnp.zeros((n, pad_hi), F32)], axis=-1)
    sin_lo = jnp.concatenate([jnp.zeros((n, QK_NOPE), F32), -sin, jnp.zeros((n, half + pad_hi), F32)], axis=-1)
    sin_hi = jnp.concatenate([jnp.zeros((n, QK_NOPE + half), F32), sin, jnp.zeros((n, pad_hi), F32)], axis=-1)
    return jnp.concatenate([cos_t, sin_lo, sin_hi], axis=-1)


def kernel(x_prompt, x_sample, cache_ckv, cache_krope, c, c_ctx, ada_w, ada_b, norm1_g, norm2_g, w_in, pool_w,
           pool_scale, q_a_norm_g, w_uq, kv_a_norm_g, w_ukv, q_norm_g, k_norm_g, w_out, router_w, router_b,
           w_gate, w_up, w_down):
    batch, seq, _ = x_prompt.shape
    dec_batch, dec_seq, _ = x_sample.shape

    conds = jnp.concatenate([c_ctx[None, :], c, jnp.zeros((8 - 1 - dec_batch, D_MODEL), F32)], axis=0)
    mod = _mod_table(conds, ada_w, ada_b).reshape(DEPTH, 8, 6, D_MODEL)

    kr_cols = jnp.pad(w_in[:, :, POOL_WIDTH + Q_RANK + KV_RANK:], ((0, 0), (0, 0), (QK_NOPE, LANE - QK_HEAD)))
    w_ukv_h = w_ukv.reshape(DEPTH, KV_RANK, N_HEADS, QK_NOPE + V_HEAD)
    weights = {
        'g1': norm1_g[:, None, :],
        'g2': norm2_g[:, None, :],
        'w_in': jnp.concatenate([w_in[:, :, :POOL_WIDTH + Q_RANK + KV_RANK], kr_cols], axis=-1).astype(BF16),
        'qag': q_a_norm_g[:, None, :],
        'kvg': kv_a_norm_g[:, None, :],
        'w_uq': _pad_heads(w_uq, QK_HEAD).astype(BF16),
        'w_uk': _pad_heads(w_ukv_h[..., :QK_NOPE].reshape(DEPTH, KV_RANK, -1), QK_NOPE).astype(BF16),
        'w_uv': _pad_heads(w_ukv_h[..., QK_NOPE:].reshape(DEPTH, KV_RANK, -1), V_HEAD).astype(BF16),
        'gq': _pad_gain(q_norm_g, 1.0 / math.sqrt(QK_HEAD)),
        'gk': _pad_gain(k_norm_g),
        'pool_w': pool_w.astype(BF16),
        'pool_scale': pool_scale[:, None, :],
        'w_out': w_out.astype(BF16),
        'w_gate': w_gate.astype(BF16),
        'w_up': w_up.astype(BF16),
        'w_down': w_down.astype(BF16),
    }
    router_wt = router_w.T.astype(BF16)
    router_bc = router_b[:, None]
    rope = _rope_table(dec_seq // GRID_W)

    cache_kr_pad = jnp.pad(cache_krope, ((0, 0), (0, 0), (0, 0), (QK_NOPE, LANE - QK_HEAD)))
    k_cache, v_cache = _cache_keys(cache_ckv, cache_kr_pad, weights['w_uk'], weights['w_uv'], weights['gk'])

    xc = x_prompt.reshape(batch * seq, D_MODEL)
    xl = x_sample.reshape(dec_batch * dec_seq, D_MODEL)
    ckv_out, kr_out = [], []
    for l in range(DEPTH):
        mod_l = mod[l]
        u, q, k, v, ckv, kr = _inproj(xc, mod_l, weights, l, latent=False, seq_len=seq)
        ckv_out.append(ckv.reshape(batch, seq, KV_RANK))
        kr_out.append(kr[:, QK_NOPE:QK_HEAD].reshape(batch, seq, QK_ROPE))
        attn = _attention(q, [(k.reshape(batch, seq, HEAD_PAD), v.reshape(batch, seq, HEAD_PAD))], seq_len=seq)
        x1, h2, gates = _post(xc, u, attn, mod_l, weights, router_wt, router_bc, l, latent=False, seq_len=seq)
        xc = _moe(h2, gates, x1, mod_l, weights, l, latent=False, seq_len=seq)
        u, q, k, v = _inproj(xl, mod_l, weights, l, latent=True, seq_len=dec_seq, rope=rope)
        attn = _attention(
            q,
            [(k_cache[l], v_cache[l]),
             (k.reshape(dec_batch, dec_seq, HEAD_PAD), v.reshape(dec_batch, dec_seq, HEAD_PAD))],
            seq_len=dec_seq)
        x1, h2, gates = _post(xl, u, attn, mod_l, weights, router_wt, router_bc, l, latent=True, seq_len=dec_seq)
        xl = _moe(h2, gates, x1, mod_l, weights, l, latent=True, seq_len=dec_seq)

    return (xc.reshape(batch, seq, D_MODEL), xl.reshape(dec_batch, dec_seq, D_MODEL),
            jnp.stack(ckv_out, axis=1), jnp.stack(kr_out, axis=1))
```

```python
import functools
import math

import jax
import jax.numpy as jnp
from jax import lax
from jax.experimental import pallas as pl
from jax.experimental.pallas import tpu as pltpu

D_MODEL = 1024
DEPTH = 4
GRID_W = 64
POOL_WINDOWS = (2, 4, 8, 16)
POOL_GROUP = 128
POOL_WIDTH = POOL_GROUP * len(POOL_WINDOWS)
N_HEADS = 8
QK_NOPE = 64
QK_ROPE = 32
QK_HEAD = QK_NOPE + QK_ROPE
V_HEAD = 64
Q_RANK = 384
KV_RANK = 256
ATTN_WIDTH = N_HEADS * V_HEAD
N_EXPERTS = 16
N_GROUPS = 4
EXPERTS_PER_GROUP = N_EXPERTS // N_GROUPS
D_EXPERT = 256
ROPE_BASE = 10000.0
EPS = 1e-6

LANE = 128
HEAD_PAD = N_HEADS * LANE
IN_PAD = POOL_WIDTH + Q_RANK + KV_RANK + LANE
POOL_HALO = max(POOL_WINDOWS)
TOKEN_BLOCK = 256
MOE_BLOCK = 1024
EXPERT_CHUNK = 4
VMEM_LIMIT = 56 * 1024 * 1024

BF16 = jnp.bfloat16
F32 = jnp.float32


def _dot(a, b):
    return jnp.dot(a, b, preferred_element_type=F32)


def _dot_nt(a, b):
    return lax.dot_general(a, b, (((1,), (1,)), ((), ())), preferred_element_type=F32)


def _rms(x, g):
    return x * lax.rsqrt(jnp.mean(x * x, axis=-1, keepdims=True) + EPS) * g


def _sigmoid(x):
    return 1.0 / (1.0 + jnp.exp(-x))


def _call(body, layer, grid, in_specs, out_specs, out_shape, args, name, scratch_shapes=()):
    return pl.pallas_call(
        body,
        grid_spec=pltpu.PrefetchScalarGridSpec(
            num_scalar_prefetch=1, grid=grid, in_specs=in_specs, out_specs=out_specs, scratch_shapes=scratch_shapes),
        out_shape=out_shape,
        compiler_params=pltpu.CompilerParams(
            dimension_semantics=("arbitrary",) * len(grid), vmem_limit_bytes=VMEM_LIMIT),
        name=name,
    )(layer, *args)


def _layer_spec(shape):
    return pl.BlockSpec((None,) + shape, lambda *a: (a[-1][0],) + (0,) * len(shape))


def _cond_spec(latent, blocks_per_seq):
    if latent:
        return pl.BlockSpec((None, None, 6, D_MODEL), lambda j, *a: (a[-1][0], 1 + j // blocks_per_seq, 0, 0))
    return pl.BlockSpec((None, None, 6, D_MODEL), lambda j, *a: (a[-1][0], 0, 0, 0))


def _mod_kernel(cond_ref, w_ref, b_ref, o_ref):
    cond = cond_ref[...]
    s = (cond * _sigmoid(cond)).astype(BF16)
    o_ref[...] = _dot(s, w_ref[...].astype(BF16)) + b_ref[...]


def _mod_table(conds, ada_w, ada_b):
    n_rows = conds.shape[0]
    tn = 1536
    return pl.pallas_call(
        _mod_kernel,
        grid=(DEPTH, 6 * D_MODEL // tn),
        in_specs=[
            pl.BlockSpec((n_rows, D_MODEL), lambda l, n: (0, 0)),
            pl.BlockSpec((None, D_MODEL, tn), lambda l, n: (l, 0, n)),
            pl.BlockSpec((None, 1, tn), lambda l, n: (l, 0, n)),
        ],
        out_specs=pl.BlockSpec((None, n_rows, tn), lambda l, n: (l, 0, n)),
        out_shape=jax.ShapeDtypeStruct((DEPTH, n_rows, 6 * D_MODEL), F32),
        compiler_params=pltpu.CompilerParams(
            dimension_semantics=("arbitrary", "arbitrary"), vmem_limit_bytes=VMEM_LIMIT),
        name="mod_table",
    )(conds, ada_w, ada_b.reshape(DEPTH, 1, 6 * D_MODEL))


def _head_norm(t, g):
    ms = jnp.sum(t * t, axis=-1, keepdims=True) * (1.0 / QK_HEAD)
    return t * lax.rsqrt(ms + EPS) * g


def _rope(t, cos, sin_lo, sin_hi):
    return t * cos + pltpu.roll(t, LANE - QK_ROPE // 2, 1) * sin_lo + pltpu.roll(t, QK_ROPE // 2, 1) * sin_hi


def _cache_keys_kernel(ckv_ref, kr_ref, wuk_ref, wuvt_ref, gk_ref, k_ref, vt_ref):
    ckv = ckv_ref[...].astype(BF16)
    kn = _dot(ckv, wuk_ref[...])
    vt_ref[...] = _dot_nt(wuvt_ref[...], ckv).astype(BF16)
    kr = kr_ref[...]
    for h in range(N_HEADS):
        sl = slice(h * LANE, (h + 1) * LANE)
        k_ref[:, sl] = _head_norm(kn[:, sl] + kr, gk_ref[...]).astype(BF16)


def _cache_keys(cache_ckv, cache_kr_pad, wuk, wuvt, gk):
    nb, _, n, _ = cache_ckv.shape
    return pl.pallas_call(
        _cache_keys_kernel,
        grid=(DEPTH, nb),
        in_specs=[
            pl.BlockSpec((None, None, n, KV_RANK), lambda l, b: (b, l, 0, 0)),
            pl.BlockSpec((None, None, n, LANE), lambda l, b: (b, l, 0, 0)),
            pl.BlockSpec((None, KV_RANK, HEAD_PAD), lambda l, b: (l, 0, 0)),
            pl.BlockSpec((None, ATTN_WIDTH, KV_RANK), lambda l, b: (l, 0, 0)),
            pl.BlockSpec((None, 1, LANE), lambda l, b: (l, 0, 0)),
        ],
        out_specs=[
            pl.BlockSpec((None, None, n, HEAD_PAD), lambda l, b: (l, b, 0, 0)),
            pl.BlockSpec((None, None, ATTN_WIDTH, n), lambda l, b: (l, b, 0, 0)),
        ],
        out_shape=[jax.ShapeDtypeStruct((DEPTH, nb, n, HEAD_PAD), BF16),
                   jax.ShapeDtypeStruct((DEPTH, nb, ATTN_WIDTH, n), BF16)],
        compiler_params=pltpu.CompilerParams(
            dimension_semantics=("arbitrary", "arbitrary"), vmem_limit_bytes=VMEM_LIMIT),
        name="cache_keys",
    )(cache_ckv, cache_kr_pad, wuk, wuvt, gk)


def _inproj_kernel(*refs, latent):
    if latent:
        (_, x_ref, mod_ref, g1_ref, win_ref, qag_ref, wuq_ref, kvg_ref, wuk_ref, wuvt_ref, gq_ref, gk_ref, rope_ref,
         u_ref, q_ref, k_ref, vt_ref) = refs
    else:
        (_, x_ref, mod_ref, g1_ref, win_ref, qag_ref, wuq_ref, kvg_ref, wuk_ref, wuvt_ref, gq_ref, gk_ref,
         u_ref, q_ref, k_ref, vt_ref, ckv_ref, kr_ref) = refs
    shift1 = mod_ref[0:1, :]
    scale1 = mod_ref[1:2, :]
    h = _rms(x_ref[...], g1_ref[...]) * (1.0 + scale1) + shift1
    proj = _dot(h.astype(BF16), win_ref[...])
    u_ref[...] = proj[:, :POOL_WIDTH]
    cq = proj[:, POOL_WIDTH:POOL_WIDTH + Q_RANK]
    q = _dot(_rms(cq, qag_ref[...]).astype(BF16), wuq_ref[...])
    ckv = _rms(proj[:, POOL_WIDTH + Q_RANK:POOL_WIDTH + Q_RANK + KV_RANK], kvg_ref[...])
    kr = proj[:, POOL_WIDTH + Q_RANK + KV_RANK:]
    if not latent:
        ckv_ref[...] = ckv
        kr_ref[...] = kr
    ckv_b = ckv.astype(BF16)
    kn = _dot(ckv_b, wuk_ref[...])
    vt_ref[...] = _dot_nt(wuvt_ref[...], ckv_b).astype(BF16)
    if latent:
        cos = rope_ref[:, 0:LANE]
        sin_lo = rope_ref[:, LANE:2 * LANE]
        sin_hi = rope_ref[:, 2 * LANE:3 * LANE]
    for hd in range(N_HEADS):
        sl = slice(hd * LANE, (hd + 1) * LANE)
        qh = _head_norm(q[:, sl], gq_ref[...])
        kh = _head_norm(kn[:, sl] + kr, gk_ref[...])
        if latent:
            qh = _rope(qh, cos, sin_lo, sin_hi)
            kh = _rope(kh, cos, sin_lo, sin_hi)
        q_ref[:, sl] = qh.astype(BF16)
        k_ref[:, sl] = kh.astype(BF16)


def _inproj(layer, x, mod, w, *, latent, seq_len, rope=None):
    t = x.shape[0]
    tb = TOKEN_BLOCK
    bps = seq_len // tb
    in_specs = [
        pl.BlockSpec((tb, D_MODEL), lambda j, l: (j, 0)),
        _cond_spec(latent, bps),
        _layer_spec((1, D_MODEL)),
        _layer_spec((D_MODEL, IN_PAD)),
        _layer_spec((1, Q_RANK)),
        _layer_spec((Q_RANK, HEAD_PAD)),
        _layer_spec((1, KV_RANK)),
        _layer_spec((KV_RANK, HEAD_PAD)),
        _layer_spec((ATTN_WIDTH, KV_RANK)),
        _layer_spec((1, LANE)),
        _layer_spec((1, LANE)),
    ]
    args = [x, mod, w['g1'], w['w_in'], w['qag'], w['w_uq'], w['kvg'], w['w_uk'], w['w_uvt'], w['gq'], w['gk']]
    tok = lambda width: pl.BlockSpec((tb, width), lambda j, l: (j, 0))
    vt_spec = pl.BlockSpec((None, ATTN_WIDTH, tb), lambda j, l: (j // bps, 0, j % bps))
    out_specs = [tok(POOL_WIDTH), tok(HEAD_PAD), tok(HEAD_PAD), vt_spec]
    out_shape = [jax.ShapeDtypeStruct((t, POOL_WIDTH), F32), jax.ShapeDtypeStruct((t, HEAD_PAD), BF16),
                 jax.ShapeDtypeStruct((t, HEAD_PAD), BF16), jax.ShapeDtypeStruct((t // seq_len, ATTN_WIDTH, seq_len), BF16)]
    if latent:
        in_specs.append(pl.BlockSpec((tb, 3 * LANE), lambda j, l: (j % bps, 0)))
        args.append(rope)
    else:
        out_specs += [tok(KV_RANK), tok(LANE)]
        out_shape += [jax.ShapeDtypeStruct((t, KV_RANK), F32), jax.ShapeDtypeStruct((t, LANE), F32)]
    return _call(functools.partial(_inproj_kernel, latent=latent), layer, (t // tb,), in_specs, out_specs, out_shape,
                 args, "inproj_latent" if latent else "inproj_context")


def _attn_kernel(*refs, n_seg):
    q_ref = refs[1]
    o_ref = refs[-1]

    def head_scores(hd):
        sl = slice(hd * LANE, (hd + 1) * LANE)
        return [_dot_nt(refs[2 + 2 * s][:, sl], q_ref[:, sl]) for s in range(n_seg)]

    outs = []
    scores = head_scores(0)
    for hd in range(N_HEADS):
        next_scores = head_scores(hd + 1) if hd + 1 < N_HEADS else None
        vs = slice(hd * V_HEAD, (hd + 1) * V_HEAD)
        m = functools.reduce(jnp.maximum, [jnp.max(st, axis=0, keepdims=True) for st in scores])
        probs = [jnp.exp2(st - m) for st in scores]
        denom = functools.reduce(jnp.add, [jnp.sum(pt, axis=0, keepdims=True) for pt in probs])
        o = functools.reduce(
            jnp.add, [_dot(refs[3 + 2 * s][vs, :], pt.astype(BF16)) for s, pt in enumerate(probs)])
        outs.append(o * (1.0 / denom))
        scores = next_scores
    o_ref[...] = jnp.concatenate(outs, axis=0).T.astype(BF16)


def _attention(layer, q, segs, *, seq_len):
    t = q.shape[0]
    tb = TOKEN_BLOCK
    bps = seq_len // tb
    in_specs = [pl.BlockSpec((tb, HEAD_PAD), lambda j, l: (j, 0))]
    args = [q]
    for k, vt, per_layer in segs:
        n = k.shape[-2]
        if per_layer:
            in_specs += [pl.BlockSpec((None, None, n, HEAD_PAD), lambda j, l: (l[0], j // bps, 0, 0)),
                         pl.BlockSpec((None, None, ATTN_WIDTH, n), lambda j, l: (l[0], j // bps, 0, 0))]
        else:
            in_specs += [pl.BlockSpec((None, n, HEAD_PAD), lambda j, l: (j // bps, 0, 0)),
                         pl.BlockSpec((None, ATTN_WIDTH, n), lambda j, l: (j // bps, 0, 0))]
        args += [k, vt]
    return _call(functools.partial(_attn_kernel, n_seg=len(segs)), layer, (t // tb,), in_specs,
                 pl.BlockSpec((tb, ATTN_WIDTH), lambda j, l: (j, 0)), jax.ShapeDtypeStruct((t, ATTN_WIDTH), BF16),
                 args, "attention_%dseg" % len(segs))


def _route(logits_t, bias):
    scores = _sigmoid(logits_t)
    sel = scores + bias
    s_rows = [scores[e:e + 1, :] for e in range(N_EXPERTS)]
    r = [sel[e:e + 1, :] for e in range(N_EXPERTS)]
    picked = []
    group_score = []
    for g in range(N_GROUPS):
        members = range(g * EXPERTS_PER_GROUP, (g + 1) * EXPERTS_PER_GROUP)
        total = None
        for e in members:
            rank = None
            for j in members:
                if j == e:
                    continue
                ahead = (r[j] >= r[e]) if j < e else (r[j] > r[e])
                ahead = ahead.astype(F32)
                rank = ahead if rank is None else rank + ahead
            pick = rank < 2.0
            picked.append(pick)
            contrib = jnp.where(pick, r[e], 0.0)
            total = contrib if total is None else total + contrib
        group_score.append(total)
    gates = []
    for g in range(N_GROUPS):
        best = None
        for j in range(N_GROUPS):
            if j == g:
                continue
            wins = (group_score[g] > group_score[j]) if j < g else (group_score[g] >= group_score[j])
            best = wins if best is None else jnp.logical_and(best, wins)
        for e in range(g * EXPERTS_PER_GROUP, (g + 1) * EXPERTS_PER_GROUP):
            gates.append(jnp.where(jnp.logical_and(best, picked[e]), s_rows[e], 0.0))
    denom = functools.reduce(jnp.add, gates)
    inv = 1.0 / denom
    return [g * inv for g in gates]


def _post_kernel(_, x_ref, up_ref, uc_ref, un_ref, attn_ref, mod_ref, poolw_ref, pscale_ref, wout_ref, g2_ref,
                 rwt_ref, rb_ref, x1_ref, h2_ref, gates_ref, uext_ref, *, blocks_per_seq, seq_len):
    tb = TOKEN_BLOCK
    halo = POOL_HALO
    jb = pl.program_id(0) % blocks_per_seq
    uext_ref[0:halo, :] = jnp.where(jb == 0, 0.0, up_ref[tb - halo:tb, :])
    uext_ref[halo:halo + tb, :] = uc_ref[...]
    uext_ref[halo + tb:, :] = jnp.where(jb == blocks_per_seq - 1, 0.0, un_ref[0:halo, :])
    pos = jb * tb + lax.broadcasted_iota(jnp.int32, (tb, 1), 0)
    pooled = []
    for g, w in enumerate(POOL_WINDOWS):
        half = w // 2
        cols = slice(g * POOL_GROUP, (g + 1) * POOL_GROUP)
        total = None
        for d in range(-half, half):
            part = uext_ref[halo + d:halo + d + tb, cols]
            total = part if total is None else total + part
        count = jnp.minimum(pos + half, seq_len) - jnp.maximum(pos - half, 0)
        diff = total / count.astype(F32) - uc_ref[:, cols]
        pooled.append(_dot(diff.astype(BF16), poolw_ref[g]))
    pool = jnp.concatenate(pooled, axis=-1) * pscale_ref[...]
    mixed = _dot(jnp.concatenate([pool.astype(BF16), attn_ref[...]], axis=-1), wout_ref[...])
    gate1 = mod_ref[2:3, :]
    shift2 = mod_ref[3:4, :]
    scale2 = mod_ref[4:5, :]
    x1 = x_ref[...] + gate1 * mixed
    x1_ref[...] = x1
    h2 = (_rms(x1, g2_ref[...]) * (1.0 + scale2) + shift2).astype(BF16)
    h2_ref[...] = h2
    gate_rows = _route(_dot_nt(rwt_ref[...], h2), rb_ref[...])
    zeros = jnp.zeros((LANE - EXPERT_CHUNK, tb), F32)
    pieces = []
    for c in range(N_EXPERTS // EXPERT_CHUNK):
        pieces += gate_rows[c * EXPERT_CHUNK:(c + 1) * EXPERT_CHUNK] + [zeros]
    gates_ref[...] = jnp.concatenate(pieces, axis=0).T


def _post(layer, x, u, attn, mod, w, router_wt, router_b, *, latent, seq_len):
    t = x.shape[0]
    tb = TOKEN_BLOCK
    bps = seq_len // tb
    n_chunks = N_EXPERTS // EXPERT_CHUNK
    n_blocks = t // tb
    tok = lambda width: pl.BlockSpec((tb, width), lambda j, l: (j, 0))
    in_specs = [
        tok(D_MODEL),
        pl.BlockSpec((tb, POOL_WIDTH), lambda j, l: (jnp.maximum(j - 1, 0), 0)),
        tok(POOL_WIDTH),
        pl.BlockSpec((tb, POOL_WIDTH), lambda j, l: (jnp.minimum(j + 1, n_blocks - 1), 0)),
        tok(ATTN_WIDTH),
        _cond_spec(latent, bps),
        _layer_spec((len(POOL_WINDOWS), POOL_GROUP, POOL_GROUP)),
        _layer_spec((1, POOL_WIDTH)),
        _layer_spec((2 * ATTN_WIDTH, D_MODEL)),
        _layer_spec((1, D_MODEL)),
        pl.BlockSpec((N_EXPERTS, D_MODEL), lambda j, l: (0, 0)),
        pl.BlockSpec((N_EXPERTS, 1), lambda j, l: (0, 0)),
    ]
    out_shape = [
        jax.ShapeDtypeStruct((t, D_MODEL), F32),
        jax.ShapeDtypeStruct((t, D_MODEL), BF16),
        jax.ShapeDtypeStruct((t, n_chunks * LANE), F32),
    ]
    return _call(functools.partial(_post_kernel, blocks_per_seq=bps, seq_len=seq_len), layer, (n_blocks,), in_specs,
                 [tok(D_MODEL), tok(D_MODEL), tok(n_chunks * LANE)], out_shape,
                 [x, u, u, u, attn, mod, w['pool_w'], w['pool_scale'], w['w_out'], w['g2'], router_wt, router_b],
                 "post_latent" if latent else "post_context",
                 scratch_shapes=[pltpu.VMEM((tb + 2 * POOL_HALO, POOL_WIDTH), F32)])


def _moe_kernel(_, h2_ref, gates_ref, x1_ref, mod_ref, wg_ref, wu_ref, wd_ref, o_ref, acc_ref):
    c = pl.program_id(1)

    @pl.when(c == 0)
    def _():
        acc_ref[...] = jnp.zeros_like(acc_ref)

    h = h2_ref[...]
    total = None
    for e in range(EXPERT_CHUNK):
        a = _dot(h, wg_ref[e])
        b = _dot(h, wu_ref[e])
        act = a * _sigmoid(a) * b * gates_ref[:, e:e + 1]
        y = _dot(act.astype(BF16), wd_ref[e])
        total = y if total is None else total + y
    acc_ref[...] += total

    @pl.when(c == pl.num_programs(1) - 1)
    def _():
        o_ref[...] = x1_ref[...] + mod_ref[5:6, :] * acc_ref[...]


def _moe(layer, h2, gates, x1, mod, w, *, latent, seq_len):
    t = h2.shape[0]
    tm = MOE_BLOCK
    bps = seq_len // tm if latent else 1
    if latent:
        cond = pl.BlockSpec((None, None, 6, D_MODEL), lambda i, c, l: (l[0], 1 + i // bps, 0, 0))
    else:
        cond = pl.BlockSpec((None, None, 6, D_MODEL), lambda i, c, l: (l[0], 0, 0, 0))
    in_specs = [
        pl.BlockSpec((tm, D_MODEL), lambda i, c, l: (i, 0)),
        pl.BlockSpec((tm, LANE), lambda i, c, l: (i, c)),
        pl.BlockSpec((tm, D_MODEL), lambda i, c, l: (i, 0)),
        cond,
        pl.BlockSpec((None, EXPERT_CHUNK, D_MODEL, D_EXPERT), lambda i, c, l: (l[0], c, 0, 0)),
        pl.BlockSpec((None, EXPERT_CHUNK, D_MODEL, D_EXPERT), lambda i, c, l: (l[0], c, 0, 0)),
        pl.BlockSpec((None, EXPERT_CHUNK, D_EXPERT, D_MODEL), lambda i, c, l: (l[0], c, 0, 0)),
    ]
    return _call(_moe_kernel, layer, (t // tm, N_EXPERTS // EXPERT_CHUNK), in_specs,
                 pl.BlockSpec((tm, D_MODEL), lambda i, c, l: (i, 0)), jax.ShapeDtypeStruct((t, D_MODEL), F32),
                 [h2, gates, x1, mod, w['w_gate'], w['w_up'], w['w_down']],
                 "experts_latent" if latent else "experts_context",
                 scratch_shapes=[pltpu.VMEM((tm, D_MODEL), F32)])


def _pad_heads(w, width):
    lead = w.shape[:-1]
    w = w.reshape(lead + (N_HEADS, width))
    w = jnp.pad(w, [(0, 0)] * len(lead) + [(0, 0), (0, LANE - width)])
    return w.reshape(lead + (HEAD_PAD,))


def _pad_gain(g, scale=1.0):
    return jnp.pad(g * scale, ((0, 0), (0, LANE - QK_HEAD)))[:, None, :]


def _rope_table(rows):
    row = jnp.repeat(jnp.arange(rows), GRID_W).astype(F32)
    col = jnp.tile(jnp.arange(GRID_W), rows).astype(F32)
    n_freq = QK_ROPE // 4
    inv = ROPE_BASE ** (-jnp.arange(n_freq, dtype=F32) / n_freq)
    ang = jnp.concatenate([row[:, None] * inv, col[:, None] * inv], axis=-1)
    cos, sin = jnp.cos(ang), jnp.sin(ang)
    n = ang.shape[0]
    half = QK_ROPE // 2
    pad_hi = LANE - QK_HEAD
    cos_t = jnp.concatenate([jnp.ones((n, QK_NOPE), F32), cos, cos, jnp.zeros((n, pad_hi), F32)], axis=-1)
    sin_lo = jnp.concatenate([jnp.zeros((n, QK_NOPE), F32), -sin, jnp.zeros((n, half + pad_hi), F32)], axis=-1)
    sin_hi = jnp.concatenate([jnp.zeros((n, QK_NOPE + half), F32), sin, jnp.zeros((n, pad_hi), F32)], axis=-1)
    return jnp.concatenate([cos_t, sin_lo, sin_hi], axis=-1)


def kernel(x_prompt, x_sample, cache_ckv, cache_krope, c, c_ctx, ada_w, ada_b, norm1_g, norm2_g, w_in, pool_w,
           pool_scale, q_a_norm_g, w_uq, kv_a_norm_g, w_ukv, q_norm_g, k_norm_g, w_out, router_w, router_b,
           w_gate, w_up, w_down):
    batch, seq, _ = x_prompt.shape
    dec_batch, dec_seq, _ = x_sample.shape

    conds = jnp.concatenate([c_ctx[None, :], c, jnp.zeros((8 - 1 - dec_batch, D_MODEL), F32)], axis=0)
    mod = _mod_table(conds, ada_w, ada_b).reshape(DEPTH, 8, 6, D_MODEL)

    kr_cols = jnp.pad(w_in[:, :, POOL_WIDTH + Q_RANK + KV_RANK:], ((0, 0), (0, 0), (QK_NOPE, LANE - QK_HEAD)))
    w_ukv_h = w_ukv.reshape(DEPTH, KV_RANK, N_HEADS, QK_NOPE + V_HEAD)
    weights = {
        'g1': norm1_g[:, None, :],
        'g2': norm2_g[:, None, :],
        'w_in': jnp.concatenate([w_in[:, :, :POOL_WIDTH + Q_RANK + KV_RANK], kr_cols], axis=-1).astype(BF16),
        'qag': q_a_norm_g[:, None, :],
        'kvg': kv_a_norm_g[:, None, :],
        'w_uq': _pad_heads(w_uq, QK_HEAD).astype(BF16),
        'w_uk': _pad_heads(w_ukv_h[..., :QK_NOPE].reshape(DEPTH, KV_RANK, -1), QK_NOPE).astype(BF16),
        'w_uvt': jnp.swapaxes(w_ukv_h[..., QK_NOPE:].reshape(DEPTH, KV_RANK, ATTN_WIDTH), 1, 2).astype(BF16),
        'gq': _pad_gain(q_norm_g, math.log2(math.e) / math.sqrt(QK_HEAD)),
        'gk': _pad_gain(k_norm_g),
        'pool_w': pool_w.astype(BF16),
        'pool_scale': pool_scale[:, None, :],
        'w_out': w_out.astype(BF16),
        'w_gate': w_gate.astype(BF16),
        'w_up': w_up.astype(BF16),
        'w_down': w_down.astype(BF16),
    }
    router_wt = router_w.T.astype(BF16)
    router_bc = router_b[:, None]
    rope = _rope_table(dec_seq // GRID_W)

    cache_kr_pad = jnp.pad(cache_krope, ((0, 0), (0, 0), (0, 0), (QK_NOPE, LANE - QK_HEAD)))
    k_cache, vt_cache = _cache_keys(cache_ckv, cache_kr_pad, weights['w_uk'], weights['w_uvt'], weights['gk'])

    def layer_step(carry, layer):
        xc, xl = carry
        u, q, k, vt, ckv, kr = _inproj(layer, xc, mod, weights, latent=False, seq_len=seq)
        ctx_seg = (k.reshape(batch, seq, HEAD_PAD), vt, False)
        attn = _attention(layer, q, [ctx_seg], seq_len=seq)
        x1, h2, gates = _post(layer, xc, u, attn, mod, weights, router_wt, router_bc, latent=False, seq_len=seq)
        xc = _moe(layer, h2, gates, x1, mod, weights, latent=False, seq_len=seq)
        u, q, k, vt = _inproj(layer, xl, mod, weights, latent=True, seq_len=dec_seq, rope=rope)
        lat_seg = (k.reshape(dec_batch, dec_seq, HEAD_PAD), vt, False)
        attn = _attention(layer, q, [(k_cache, vt_cache, True), lat_seg], seq_len=dec_seq)
        x1, h2, gates = _post(layer, xl, u, attn, mod, weights, router_wt, router_bc, latent=True, seq_len=dec_seq)
        xl = _moe(layer, h2, gates, x1, mod, weights, latent=True, seq_len=dec_seq)
        return (xc, xl), (ckv, kr)

    layers = jnp.arange(DEPTH, dtype=jnp.int32).reshape(DEPTH, 1)
    (xc, xl), (ckv_all, kr_all) = lax.scan(
        layer_step, (x_prompt.reshape(batch * seq, D_MODEL), x_sample.reshape(dec_batch * dec_seq, D_MODEL)), layers)

    new_ckv = jnp.swapaxes(ckv_all.reshape(DEPTH, batch, seq, KV_RANK), 0, 1)
    new_krope = jnp.swapaxes(kr_all[:, :, QK_NOPE:QK_HEAD].reshape(DEPTH, batch, seq, QK_ROPE), 0, 1)
    return (xc.reshape(batch, seq, D_MODEL), xl.reshape(dec_batch, dec_seq, D_MODEL), new_ckv, new_krope)
```

```python
import functools
import math

import jax
import jax.numpy as jnp
from jax import lax
from jax.experimental import pallas as pl
from jax.experimental.pallas import tpu as pltpu

D_MODEL = 1024
DEPTH = 4
GRID_W = 64
POOL_WINDOWS = (2, 4, 8, 16)
POOL_GROUP = 128
POOL_WIDTH = POOL_GROUP * len(POOL_WINDOWS)
N_HEADS = 8
QK_NOPE = 64
QK_ROPE = 32
QK_HEAD = QK_NOPE + QK_ROPE
V_HEAD = 64
Q_RANK = 384
KV_RANK = 256
ATTN_WIDTH = N_HEADS * V_HEAD
N_EXPERTS = 16
N_GROUPS = 4
EXPERTS_PER_GROUP = N_EXPERTS // N_GROUPS
D_EXPERT = 256
ROPE_BASE = 10000.0
EPS = 1e-6

LANE = 128
HEAD_PAD = N_HEADS * LANE
ONES_ROWS = 16
V_ROWS = V_HEAD + ONES_ROWS
VT_ROWS = N_HEADS * V_ROWS
POOL_HALO = max(POOL_WINDOWS)
TOKEN_BLOCK = 256
MOE_BLOCK = 1024
EXPERT_CHUNK = 4
VMEM_LIMIT = 56 * 1024 * 1024

BF16 = jnp.bfloat16
F32 = jnp.float32


def _dot(a, b):
    return jnp.dot(a, b, preferred_element_type=F32)


def _dot_nt(a, b):
    return lax.dot_general(a, b, (((1,), (1,)), ((), ())), preferred_element_type=F32)


def _rms(x, g):
    return x * lax.rsqrt(jnp.mean(x * x, axis=-1, keepdims=True) + EPS) * g


def _sigmoid(x):
    return 1.0 / (1.0 + jnp.exp(-x))


def _call(body, layer, grid, in_specs, out_specs, out_shape, args, name, scratch_shapes=()):
    return pl.pallas_call(
        body,
        grid_spec=pltpu.PrefetchScalarGridSpec(
            num_scalar_prefetch=1, grid=grid, in_specs=in_specs, out_specs=out_specs, scratch_shapes=scratch_shapes),
        out_shape=out_shape,
        compiler_params=pltpu.CompilerParams(
            dimension_semantics=("arbitrary",) * len(grid), vmem_limit_bytes=VMEM_LIMIT),
        name=name,
    )(layer, *args)


def _layer_spec(shape):
    return pl.BlockSpec((None,) + shape, lambda *a: (a[-1][0],) + (0,) * len(shape))


def _cond_spec(latent, blocks_per_seq):
    if latent:
        return pl.BlockSpec((None, None, 6, D_MODEL), lambda j, *a: (a[-1][0], 1 + j // blocks_per_seq, 0, 0))
    return pl.BlockSpec((None, None, 6, D_MODEL), lambda j, *a: (a[-1][0], 0, 0, 0))


def _mod_kernel(cond_ref, w_ref, b_ref, o_ref):
    cond = cond_ref[...]
    s = (cond * _sigmoid(cond)).astype(BF16)
    o_ref[...] = _dot(s, w_ref[...].astype(BF16)) + b_ref[...]


def _mod_table(conds, ada_w, ada_b):
    n_rows = conds.shape[0]
    tn = 1536
    return pl.pallas_call(
        _mod_kernel,
        grid=(DEPTH, 6 * D_MODEL // tn),
        in_specs=[
            pl.BlockSpec((n_rows, D_MODEL), lambda l, n: (0, 0)),
            pl.BlockSpec((None, D_MODEL, tn), lambda l, n: (l, 0, n)),
            pl.BlockSpec((None, 1, tn), lambda l, n: (l, 0, n)),
        ],
        out_specs=pl.BlockSpec((None, n_rows, tn), lambda l, n: (l, 0, n)),
        out_shape=jax.ShapeDtypeStruct((DEPTH, n_rows, 6 * D_MODEL), F32),
        compiler_params=pltpu.CompilerParams(
            dimension_semantics=("arbitrary", "arbitrary"), vmem_limit_bytes=VMEM_LIMIT),
        name="mod_table",
    )(conds, ada_w, ada_b.reshape(DEPTH, 1, 6 * D_MODEL))


def _head_norm(t, g):
    ms = jnp.sum(t * t, axis=-1, keepdims=True) * (1.0 / QK_HEAD)
    return t * lax.rsqrt(ms + EPS) * g


def _value_rows(vt):
    ones = jnp.ones((ONES_ROWS, vt.shape[1]), BF16)
    parts = []
    for h in range(N_HEADS):
        parts += [vt[h * V_HEAD:(h + 1) * V_HEAD, :].astype(BF16), ones]
    return jnp.concatenate(parts, axis=0)


def _cache_keys_kernel(ckv_ref, kr_ref, wuk_ref, wuvt_ref, gk_ref, k_ref, vt_ref):
    ckv = ckv_ref[...].astype(BF16)
    kn = _dot(ckv, wuk_ref[...])
    vt_ref[...] = _value_rows(_dot_nt(wuvt_ref[...], ckv))
    kr = kr_ref[...]
    for h in range(N_HEADS):
        sl = slice(h * LANE, (h + 1) * LANE)
        k_ref[:, sl] = _head_norm(kn[:, sl] + kr, gk_ref[...]).astype(BF16)


def _cache_keys(cache_ckv, cache_kr_pad, wuk, wuvt, gk):
    nb, _, n, _ = cache_ckv.shape
    return pl.pallas_call(
        _cache_keys_kernel,
        grid=(DEPTH, nb),
        in_specs=[
            pl.BlockSpec((None, None, n, KV_RANK), lambda l, b: (b, l, 0, 0)),
            pl.BlockSpec((None, None, n, LANE), lambda l, b: (b, l, 0, 0)),
            pl.BlockSpec((None, KV_RANK, HEAD_PAD), lambda l, b: (l, 0, 0)),
            pl.BlockSpec((None, ATTN_WIDTH, KV_RANK), lambda l, b: (l, 0, 0)),
            pl.BlockSpec((None, 1, LANE), lambda l, b: (l, 0, 0)),
        ],
        out_specs=[
            pl.BlockSpec((None, None, n, HEAD_PAD), lambda l, b: (l, b, 0, 0)),
            pl.BlockSpec((None, None, VT_ROWS, n), lambda l, b: (l, b, 0, 0)),
        ],
        out_shape=[jax.ShapeDtypeStruct((DEPTH, nb, n, HEAD_PAD), BF16),
                   jax.ShapeDtypeStruct((DEPTH, nb, VT_ROWS, n), BF16)],
        compiler_params=pltpu.CompilerParams(
            dimension_semantics=("arbitrary", "arbitrary"), vmem_limit_bytes=VMEM_LIMIT),
        name="cache_keys",
    )(cache_ckv, cache_kr_pad, wuk, wuvt, gk)


def _lanes(col_ref, n):
    return jnp.concatenate([col_ref[...]] * (n // LANE), axis=1)


def _rms_t(t, g):
    return t * lax.rsqrt(jnp.mean(t * t, axis=0, keepdims=True) + EPS) * g


def _head_t(t, g, rope):
    ms = jnp.sum(t * t, axis=0, keepdims=True) * (1.0 / QK_HEAD)
    t = t * lax.rsqrt(ms + EPS) * g
    if rope is not None:
        cos, sin = rope
        half = QK_ROPE // 2
        r1 = t[QK_NOPE:QK_NOPE + half, :]
        r2 = t[QK_NOPE + half:QK_HEAD, :]
        t = jnp.concatenate([t[:QK_NOPE, :], r1 * cos - r2 * sin, r1 * sin + r2 * cos, t[QK_HEAD:, :]], axis=0)
    return t


def _inproj_kernel(*refs, latent):
    if latent:
        (_, x_ref, mod_ref, g1_ref, wu_ref, wrt_ref, qag_ref, wuqt_ref, kvg_ref, wukt_ref, wuvt_ref, gq_ref, gk_ref,
         rope_ref, u_ref, q_ref, k_ref, vt_ref) = refs
    else:
        (_, x_ref, mod_ref, g1_ref, wu_ref, wrt_ref, qag_ref, wuqt_ref, kvg_ref, wukt_ref, wuvt_ref, gq_ref, gk_ref,
         u_ref, q_ref, k_ref, vt_ref, ckv_ref, kr_ref) = refs
    n = TOKEN_BLOCK
    shift1 = mod_ref[0:1, :]
    scale1 = mod_ref[1:2, :]
    h = (_rms(x_ref[...], g1_ref[...]) * (1.0 + scale1) + shift1).astype(BF16)
    u_ref[...] = _dot(h, wu_ref[...])
    pt = _dot_nt(wrt_ref[...], h)
    cq = _rms_t(pt[:Q_RANK, :], _lanes(qag_ref, n))
    qt = _dot(wuqt_ref[...], cq.astype(BF16))
    ckv = _rms_t(pt[Q_RANK:Q_RANK + KV_RANK, :], _lanes(kvg_ref, n))
    krt = pt[Q_RANK + KV_RANK:, :]
    ckv_b = ckv.astype(BF16)
    knt = _dot(wukt_ref[...], ckv_b)
    vt_ref[...] = _value_rows(_dot(wuvt_ref[...], ckv_b))
    rope = (rope_ref[0:QK_ROPE // 2, :], rope_ref[QK_ROPE // 2:QK_ROPE, :]) if latent else None
    gq = _lanes(gq_ref, n)
    gk = _lanes(gk_ref, n)
    zeros = jnp.zeros((LANE - QK_HEAD, n), F32)
    k_heads = []
    for hd in range(N_HEADS):
        qh = jnp.concatenate([qt[hd * QK_HEAD:(hd + 1) * QK_HEAD, :], zeros], axis=0)
        q_ref[hd * LANE:(hd + 1) * LANE, :] = _head_t(qh, gq, rope).astype(BF16)
        kh = jnp.concatenate([knt[hd * QK_NOPE:(hd + 1) * QK_NOPE, :], krt, zeros], axis=0)
        k_heads.append(_head_t(kh, gk, rope))
    k_ref[...] = jnp.concatenate(k_heads, axis=0).T.astype(BF16)
    if not latent:
        ckv_ref[...] = ckv.T
        kr_ref[...] = jnp.concatenate([jnp.zeros((QK_NOPE, n), F32), krt, zeros], axis=0).T


def _inproj(layer, x, mod, w, *, latent, seq_len, rope=None):
    t = x.shape[0]
    tb = TOKEN_BLOCK
    bps = seq_len // tb
    rest = Q_RANK + KV_RANK + QK_ROPE
    in_specs = [
        pl.BlockSpec((tb, D_MODEL), lambda j, l: (j, 0)),
        _cond_spec(latent, bps),
        _layer_spec((1, D_MODEL)),
        _layer_spec((D_MODEL, POOL_WIDTH)),
        _layer_spec((rest, D_MODEL)),
        _layer_spec((Q_RANK, LANE)),
        _layer_spec((N_HEADS * QK_HEAD, Q_RANK)),
        _layer_spec((KV_RANK, LANE)),
        _layer_spec((N_HEADS * QK_NOPE, KV_RANK)),
        _layer_spec((ATTN_WIDTH, KV_RANK)),
        _layer_spec((LANE, LANE)),
        _layer_spec((LANE, LANE)),
    ]
    args = [x, mod, w['g1'], w['w_u'], w['w_rest_t'], w['qag_col'], w['w_uq_t'], w['kvg_col'], w['w_uk_t'],
            w['w_uvt'], w['gq_col'], w['gk_col']]
    tok = lambda width: pl.BlockSpec((tb, width), lambda j, l: (j, 0))
    vt_spec = pl.BlockSpec((None, VT_ROWS, tb), lambda j, l: (j // bps, 0, j % bps))
    out_specs = [tok(POOL_WIDTH), pl.BlockSpec((HEAD_PAD, tb), lambda j, l: (0, j)), tok(HEAD_PAD), vt_spec]
    out_shape = [jax.ShapeDtypeStruct((t, POOL_WIDTH), F32), jax.ShapeDtypeStruct((HEAD_PAD, t), BF16),
                 jax.ShapeDtypeStruct((t, HEAD_PAD), BF16), jax.ShapeDtypeStruct((t // seq_len, VT_ROWS, seq_len), BF16)]
    if latent:
        in_specs.append(pl.BlockSpec((QK_ROPE, tb), lambda j, l: (0, j % bps)))
        args.append(rope)
    else:
        out_specs += [tok(KV_RANK), tok(LANE)]
        out_shape += [jax.ShapeDtypeStruct((t, KV_RANK), F32), jax.ShapeDtypeStruct((t, LANE), F32)]
    return _call(functools.partial(_inproj_kernel, latent=latent), layer, (t // tb,), in_specs, out_specs, out_shape,
                 args, "inproj_latent" if latent else "inproj_context")


def _attn_kernel(*refs, n_seg):
    q_ref = refs[1]
    o_ref = refs[-1]
    tb = TOKEN_BLOCK
    tiles = [(s, c) for s in range(n_seg) for c in range(refs[2 + 2 * s].shape[0] // tb)]

    def fold(x):
        return x.reshape(x.shape[0] // 8, 8, x.shape[1])

    def score_tile(hd, s, c):
        sl = slice(hd * LANE, (hd + 1) * LANE)
        st = _dot(refs[2 + 2 * s][c * tb:(c + 1) * tb, sl], q_ref[sl, :])
        return st, jnp.max(fold(st), axis=0)

    def running_max(m8, mt):
        return mt if m8 is None else jnp.maximum(m8, mt)

    st_cur, m8 = [], None
    for s, c in tiles:
        st, mt = score_tile(0, s, c)
        st_cur.append(st)
        m8 = running_max(m8, mt)
    outs = []
    for hd in range(N_HEADS):
        vs = slice(hd * V_ROWS, (hd + 1) * V_ROWS)
        m = jnp.max(m8, axis=0, keepdims=True)
        st_next, m8 = [], None
        acc = None
        pending = None

        def pv_step(acc, pending):
            pb, s, c = pending
            pv = _dot(refs[3 + 2 * s][vs, c * tb:(c + 1) * tb], pb)
            return pv if acc is None else acc + pv

        for i, (s, c) in enumerate(tiles):
            if hd + 1 < N_HEADS:
                st, mt = score_tile(hd + 1, s, c)
                st_next.append(st)
                m8 = running_max(m8, mt)
            if pending is not None:
                acc = pv_step(acc, pending)
            pending = (jnp.exp2((st_cur[i] - m).astype(BF16)), s, c)
        acc = pv_step(acc, pending)
        outs.append(acc[:V_HEAD, :] * (1.0 / acc[V_HEAD:V_HEAD + 1, :]))
        st_cur = st_next
    o_ref[...] = jnp.concatenate(outs, axis=0).T.astype(BF16)


def _attention(layer, q, segs, *, seq_len):
    t = q.shape[1]
    tb = TOKEN_BLOCK
    bps = seq_len // tb
    in_specs = [pl.BlockSpec((HEAD_PAD, tb), lambda j, l: (0, j))]
    args = [q]
    for k, vt, per_layer in segs:
        n = k.shape[-2]
        if per_layer:
            in_specs += [pl.BlockSpec((None, None, n, HEAD_PAD), lambda j, l: (l[0], j // bps, 0, 0)),
                         pl.BlockSpec((None, None, VT_ROWS, n), lambda j, l: (l[0], j // bps, 0, 0))]
        else:
            in_specs += [pl.BlockSpec((None, n, HEAD_PAD), lambda j, l: (j // bps, 0, 0)),
                         pl.BlockSpec((None, VT_ROWS, n), lambda j, l: (j // bps, 0, 0))]
        args += [k, vt]
    return _call(functools.partial(_attn_kernel, n_seg=len(segs)), layer, (t // tb,), in_specs,
                 pl.BlockSpec((tb, ATTN_WIDTH), lambda j, l: (j, 0)), jax.ShapeDtypeStruct((t, ATTN_WIDTH), BF16),
                 args, "attention_%dseg" % len(segs))


def _route(logits_t, bias):
    scores = _sigmoid(logits_t)
    sel = scores + bias
    s_rows = [scores[e:e + 1, :] for e in range(N_EXPERTS)]
    r = [sel[e:e + 1, :] for e in range(N_EXPERTS)]
    picked = []
    group_score = []
    for g in range(N_GROUPS):
        members = range(g * EXPERTS_PER_GROUP, (g + 1) * EXPERTS_PER_GROUP)
        total = None
        for e in members:
            rank = None
            for j in members:
                if j == e:
                    continue
                ahead = (r[j] >= r[e]) if j < e else (r[j] > r[e])
                ahead = ahead.astype(F32)
                rank = ahead if rank is None else rank + ahead
            pick = rank < 2.0
            picked.append(pick)
            contrib = jnp.where(pick, r[e], 0.0)
            total = contrib if total is None else total + contrib
        group_score.append(total)
    gates = []
    for g in range(N_GROUPS):
        best = None
        for j in range(N_GROUPS):
            if j == g:
                continue
            wins = (group_score[g] > group_score[j]) if j < g else (group_score[g] >= group_score[j])
            best = wins if best is None else jnp.logical_and(best, wins)
        for e in range(g * EXPERTS_PER_GROUP, (g + 1) * EXPERTS_PER_GROUP):
            gates.append(jnp.where(jnp.logical_and(best, picked[e]), s_rows[e], 0.0))
    denom = functools.reduce(jnp.add, gates)
    inv = 1.0 / denom
    return [g * inv for g in gates]


def _post_kernel(_, x_ref, up_ref, uc_ref, un_ref, attn_ref, mod_ref, poolw_ref, pscale_ref, wout_ref, g2_ref,
                 rwt_ref, rb_ref, x1_ref, h2_ref, gates_ref, uext_ref, *, blocks_per_seq, seq_len):
    tb = TOKEN_BLOCK
    halo = POOL_HALO
    jb = pl.program_id(0) % blocks_per_seq
    uext_ref[0:halo, :] = jnp.where(jb == 0, 0.0, up_ref[tb - halo:tb, :])
    uext_ref[halo:halo + tb, :] = uc_ref[...]
    uext_ref[halo + tb:, :] = jnp.where(jb == blocks_per_seq - 1, 0.0, un_ref[0:halo, :])
    pos = jb * tb + lax.broadcasted_iota(jnp.int32, (tb, 1), 0)
    pooled = []
    for g, w in enumerate(POOL_WINDOWS):
        half = w // 2
        cols = slice(g * POOL_GROUP, (g + 1) * POOL_GROUP)
        total = None
        for d in range(-half, half):
            part = uext_ref[halo + d:halo + d + tb, cols]
            total = part if total is None else total + part
        count = jnp.minimum(pos + half, seq_len) - jnp.maximum(pos - half, 0)
        diff = total / count.astype(F32) - uc_ref[:, cols]
        pooled.append(_dot(diff.astype(BF16), poolw_ref[g]))
    pool = jnp.concatenate(pooled, axis=-1) * pscale_ref[...]
    mixed = _dot(jnp.concatenate([pool.astype(BF16), attn_ref[...]], axis=-1), wout_ref[...])
    gate1 = mod_ref[2:3, :]
    shift2 = mod_ref[3:4, :]
    scale2 = mod_ref[4:5, :]
    x1 = x_ref[...] + gate1 * mixed
    x1_ref[...] = x1
    h2 = (_rms(x1, g2_ref[...]) * (1.0 + scale2) + shift2).astype(BF16)
    h2_ref[...] = h2
    gate_rows = _route(_dot_nt(rwt_ref[...], h2), rb_ref[...])
    zeros = jnp.zeros((LANE - EXPERT_CHUNK, tb), F32)
    pieces = []
    for c in range(N_EXPERTS // EXPERT_CHUNK):
        pieces += gate_rows[c * EXPERT_CHUNK:(c + 1) * EXPERT_CHUNK] + [zeros]
    gates_ref[...] = jnp.concatenate(pieces, axis=0).T


def _post(layer, x, u, attn, mod, w, router_wt, router_b, *, latent, seq_len):
    t = x.shape[0]
    tb = TOKEN_BLOCK
    bps = seq_len // tb
    n_chunks = N_EXPERTS // EXPERT_CHUNK
    n_blocks = t // tb
    tok = lambda width: pl.BlockSpec((tb, width), lambda j, l: (j, 0))
    in_specs = [
        tok(D_MODEL),
        pl.BlockSpec((tb, POOL_WIDTH), lambda j, l: (jnp.maximum(j - 1, 0), 0)),
        tok(POOL_WIDTH),
        pl.BlockSpec((tb, POOL_WIDTH), lambda j, l: (jnp.minimum(j + 1, n_blocks - 1), 0)),
        tok(ATTN_WIDTH),
        _cond_spec(latent, bps),
        _layer_spec((len(POOL_WINDOWS), POOL_GROUP, POOL_GROUP)),
        _layer_spec((1, POOL_WIDTH)),
        _layer_spec((2 * ATTN_WIDTH, D_MODEL)),
        _layer_spec((1, D_MODEL)),
        pl.BlockSpec((N_EXPERTS, D_MODEL), lambda j, l: (0, 0)),
        pl.BlockSpec((N_EXPERTS, 1), lambda j, l: (0, 0)),
    ]
    out_shape = [
        jax.ShapeDtypeStruct((t, D_MODEL), F32),
        jax.ShapeDtypeStruct((t, D_MODEL), BF16),
        jax.ShapeDtypeStruct((t, n_chunks * LANE), F32),
    ]
    return _call(functools.partial(_post_kernel, blocks_per_seq=bps, seq_len=seq_len), layer, (n_blocks,), in_specs,
                 [tok(D_MODEL), tok(D_MODEL), tok(n_chunks * LANE)], out_shape,
                 [x, u, u, u, attn, mod, w['pool_w'], w['pool_scale'], w['w_out'], w['g2'], router_wt, router_b],
                 "post_latent" if latent else "post_context",
                 scratch_shapes=[pltpu.VMEM((tb + 2 * POOL_HALO, POOL_WIDTH), F32)])


def _moe_kernel(_, h2_ref, gates_ref, x1_ref, mod_ref, wg_ref, wu_ref, wd_ref, o_ref, acc_ref):
    c = pl.program_id(1)

    @pl.when(c == 0)
    def _():
        acc_ref[...] = jnp.zeros_like(acc_ref)

    h = h2_ref[...]
    total = None
    for e in range(EXPERT_CHUNK):
        a = _dot(h, wg_ref[e])
        b = _dot(h, wu_ref[e])
        act = a * _sigmoid(a) * b * gates_ref[:, e:e + 1]
        y = _dot(act.astype(BF16), wd_ref[e])
        total = y if total is None else total + y
    acc_ref[...] += total

    @pl.when(c == pl.num_programs(1) - 1)
    def _():
        o_ref[...] = x1_ref[...] + mod_ref[5:6, :] * acc_ref[...]


def _moe(layer, h2, gates, x1, mod, w, *, latent, seq_len):
    t = h2.shape[0]
    tm = MOE_BLOCK
    bps = seq_len // tm if latent else 1
    if latent:
        cond = pl.BlockSpec((None, None, 6, D_MODEL), lambda i, c, l: (l[0], 1 + i // bps, 0, 0))
    else:
        cond = pl.BlockSpec((None, None, 6, D_MODEL), lambda i, c, l: (l[0], 0, 0, 0))
    in_specs = [
        pl.BlockSpec((tm, D_MODEL), lambda i, c, l: (i, 0)),
        pl.BlockSpec((tm, LANE), lambda i, c, l: (i, c)),
        pl.BlockSpec((tm, D_MODEL), lambda i, c, l: (i, 0)),
        cond,
        pl.BlockSpec((None, EXPERT_CHUNK, D_MODEL, D_EXPERT), lambda i, c, l: (l[0], c, 0, 0)),
        pl.BlockSpec((None, EXPERT_CHUNK, D_MODEL, D_EXPERT), lambda i, c, l: (l[0], c, 0, 0)),
        pl.BlockSpec((None, EXPERT_CHUNK, D_EXPERT, D_MODEL), lambda i, c, l: (l[0], c, 0, 0)),
    ]
    return _call(_moe_kernel, layer, (t // tm, N_EXPERTS // EXPERT_CHUNK), in_specs,
                 pl.BlockSpec((tm, D_MODEL), lambda i, c, l: (i, 0)), jax.ShapeDtypeStruct((t, D_MODEL), F32),
                 [h2, gates, x1, mod, w['w_gate'], w['w_up'], w['w_down']],
                 "experts_latent" if latent else "experts_context",
                 scratch_shapes=[pltpu.VMEM((tm, D_MODEL), F32)])


def _pad_heads(w, width):
    lead = w.shape[:-1]
    w = w.reshape(lead + (N_HEADS, width))
    w = jnp.pad(w, [(0, 0)] * len(lead) + [(0, 0), (0, LANE - width)])
    return w.reshape(lead + (HEAD_PAD,))


def _pad_gain(g):
    return jnp.pad(g, ((0, 0), (0, LANE - QK_HEAD)))[:, None, :]


def _rope_table(rows):
    row = jnp.repeat(jnp.arange(rows), GRID_W).astype(F32)
    col = jnp.tile(jnp.arange(GRID_W), rows).astype(F32)
    n_freq = QK_ROPE // 4
    inv = ROPE_BASE ** (-jnp.arange(n_freq, dtype=F32) / n_freq)
    ang = jnp.concatenate([row[:, None] * inv, col[:, None] * inv], axis=-1)
    return jnp.concatenate([jnp.cos(ang), jnp.sin(ang)], axis=-1).T


def _col(g, rows):
    g = jnp.pad(g, ((0, 0), (0, rows - g.shape[1])))
    return jnp.broadcast_to(g[:, :, None], g.shape + (LANE,))


def kernel(x_prompt, x_sample, cache_ckv, cache_krope, c, c_ctx, ada_w, ada_b, norm1_g, norm2_g, w_in, pool_w,
           pool_scale, q_a_norm_g, w_uq, kv_a_norm_g, w_ukv, q_norm_g, k_norm_g, w_out, router_w, router_b,
           w_gate, w_up, w_down):
    batch, seq, _ = x_prompt.shape
    dec_batch, dec_seq, _ = x_sample.shape

    conds = jnp.concatenate([c_ctx[None, :], c, jnp.zeros((8 - 1 - dec_batch, D_MODEL), F32)], axis=0)
    mod = _mod_table(conds, ada_w, ada_b).reshape(DEPTH, 8, 6, D_MODEL)

    w_ukv_h = w_ukv.reshape(DEPTH, KV_RANK, N_HEADS, QK_NOPE + V_HEAD)
    w_uk = w_ukv_h[..., :QK_NOPE].reshape(DEPTH, KV_RANK, N_HEADS * QK_NOPE)
    q_scale = math.log2(math.e) / math.sqrt(QK_HEAD)
    weights = {
        'g1': norm1_g[:, None, :],
        'g2': norm2_g[:, None, :],
        'w_u': w_in[:, :, :POOL_WIDTH].astype(BF16),
        'w_rest_t': jnp.swapaxes(w_in[:, :, POOL_WIDTH:], 1, 2).astype(BF16),
        'qag_col': _col(q_a_norm_g, Q_RANK),
        'kvg_col': _col(kv_a_norm_g, KV_RANK),
        'w_uq_t': jnp.swapaxes(w_uq, 1, 2).astype(BF16),
        'w_uk_t': jnp.swapaxes(w_uk, 1, 2).astype(BF16),
        'w_uk': _pad_heads(w_uk, QK_NOPE).astype(BF16),
        'w_uvt': jnp.swapaxes(w_ukv_h[..., QK_NOPE:].reshape(DEPTH, KV_RANK, ATTN_WIDTH), 1, 2).astype(BF16),
        'gq_col': _col(q_norm_g * q_scale, LANE),
        'gk_col': _col(k_norm_g, LANE),
        'gk': _pad_gain(k_norm_g),
        'pool_w': pool_w.astype(BF16),
        'pool_scale': pool_scale[:, None, :],
        'w_out': w_out.astype(BF16),
        'w_gate': w_gate.astype(BF16),
        'w_up': w_up.astype(BF16),
        'w_down': w_down.astype(BF16),
    }
    router_wt = router_w.T.astype(BF16)
    router_bc = router_b[:, None]
    rope = _rope_table(dec_seq // GRID_W)

    cache_kr_pad = jnp.pad(cache_krope, ((0, 0), (0, 0), (0, 0), (QK_NOPE, LANE - QK_HEAD)))
    k_cache, vt_cache = _cache_keys(cache_ckv, cache_kr_pad, weights['w_uk'], weights['w_uvt'], weights['gk'])

    def layer_step(carry, layer):
        xc, xl = carry
        u, q, k, vt, ckv, kr = _inproj(layer, xc, mod, weights, latent=False, seq_len=seq)
        ctx_seg = (k.reshape(batch, seq, HEAD_PAD), vt, False)
        attn = _attention(layer, q, [ctx_seg], seq_len=seq)
        x1, h2, gates = _post(layer, xc, u, attn, mod, weights, router_wt, router_bc, latent=False, seq_len=seq)
        xc = _moe(layer, h2, gates, x1, mod, weights, latent=False, seq_len=seq)
        u, q, k, vt = _inproj(layer, xl, mod, weights, latent=True, seq_len=dec_seq, rope=rope)
        lat_seg = (k.reshape(dec_batch, dec_seq, HEAD_PAD), vt, False)
        attn = _attention(layer, q, [(k_cache, vt_cache, True), lat_seg], seq_len=dec_seq)
        x1, h2, gates = _post(layer, xl, u, attn, mod, weights, router_wt, router_bc, latent=True, seq_len=dec_seq)
        xl = _moe(layer, h2, gates, x1, mod, weights, latent=True, seq_len=dec_seq)
        return (xc, xl), (ckv, kr)

    layers = jnp.arange(DEPTH, dtype=jnp.int32).reshape(DEPTH, 1)
    (xc, xl), (ckv_all, kr_all) = lax.scan(
        layer_step, (x_prompt.reshape(batch * seq, D_MODEL), x_sample.reshape(dec_batch * dec_seq, D_MODEL)), layers)

    new_ckv = jnp.swapaxes(ckv_all.reshape(DEPTH, batch, seq, KV_RANK), 0, 1)
    new_krope = jnp.swapaxes(kr_all[:, :, QK_NOPE:QK_HEAD].reshape(DEPTH, batch, seq, QK_ROPE), 0, 1)
    return (xc.reshape(batch, seq, D_MODEL), xl.reshape(dec_batch, dec_seq, D_MODEL), new_ckv, new_krope)
```

```python
import functools
import math

import jax
import jax.numpy as jnp
from jax import lax
from jax.experimental import pallas as pl
from jax.experimental.pallas import tpu as pltpu

D_MODEL = 1024
DEPTH = 4
GRID_W = 64
POOL_WINDOWS = (2, 4, 8, 16)
POOL_GROUP = 128
POOL_WIDTH = POOL_GROUP * len(POOL_WINDOWS)
N_HEADS = 8
QK_NOPE = 64
QK_ROPE = 32
QK_HEAD = QK_NOPE + QK_ROPE
V_HEAD = 64
Q_RANK = 384
KV_RANK = 256
ATTN_WIDTH = N_HEADS * V_HEAD
N_EXPERTS = 16
N_GROUPS = 4
EXPERTS_PER_GROUP = N_EXPERTS // N_GROUPS
D_EXPERT = 256
ROPE_BASE = 10000.0
EPS = 1e-6

LANE = 128
HEAD_PAD = N_HEADS * LANE
ONES_ROWS = 16
V_ROWS = V_HEAD + ONES_ROWS
VT_ROWS = N_HEADS * V_ROWS
POOL_HALO = max(POOL_WINDOWS)
TOKEN_BLOCK = 256
PROJ_BLOCK = 512
MOE_BLOCK = 1024
EXPERT_CHUNK = 4
VMEM_LIMIT = 56 * 1024 * 1024

BF16 = jnp.bfloat16
F32 = jnp.float32


def _dot(a, b):
    return jnp.dot(a, b, preferred_element_type=F32)


def _dot_nt(a, b):
    return lax.dot_general(a, b, (((1,), (1,)), ((), ())), preferred_element_type=F32)


def _rms(x, g):
    return x * lax.rsqrt(jnp.mean(x * x, axis=-1, keepdims=True) + EPS) * g


def _sigmoid(x):
    return 1.0 / (1.0 + jnp.exp(-x))


def _call(body, layer, grid, in_specs, out_specs, out_shape, args, name, scratch_shapes=(), flags=None):
    return pl.pallas_call(
        body,
        grid_spec=pltpu.PrefetchScalarGridSpec(
            num_scalar_prefetch=1, grid=grid, in_specs=in_specs, out_specs=out_specs, scratch_shapes=scratch_shapes),
        out_shape=out_shape,
        compiler_params=pltpu.CompilerParams(
            dimension_semantics=("arbitrary",) * len(grid), vmem_limit_bytes=VMEM_LIMIT, flags=flags),
        name=name,
    )(layer, *args)


def _layer_spec(shape):
    return pl.BlockSpec((None,) + shape, lambda *a: (a[-1][0],) + (0,) * len(shape))


def _cond_spec(latent, blocks_per_seq):
    if latent:
        return pl.BlockSpec((None, None, 6, D_MODEL), lambda j, *a: (a[-1][0], 1 + j // blocks_per_seq, 0, 0))
    return pl.BlockSpec((None, None, 6, D_MODEL), lambda j, *a: (a[-1][0], 0, 0, 0))


def _mod_kernel(cond_ref, w_ref, b_ref, o_ref):
    cond = cond_ref[...]
    s = (cond * _sigmoid(cond)).astype(BF16)
    o_ref[...] = _dot(s, w_ref[...].astype(BF16)) + b_ref[...]


def _mod_table(conds, ada_w, ada_b):
    n_rows = conds.shape[0]
    tn = 1536
    return pl.pallas_call(
        _mod_kernel,
        grid=(DEPTH, 6 * D_MODEL // tn),
        in_specs=[
            pl.BlockSpec((n_rows, D_MODEL), lambda l, n: (0, 0)),
            pl.BlockSpec((None, D_MODEL, tn), lambda l, n: (l, 0, n)),
            pl.BlockSpec((None, 1, tn), lambda l, n: (l, 0, n)),
        ],
        out_specs=pl.BlockSpec((None, n_rows, tn), lambda l, n: (l, 0, n)),
        out_shape=jax.ShapeDtypeStruct((DEPTH, n_rows, 6 * D_MODEL), F32),
        compiler_params=pltpu.CompilerParams(
            dimension_semantics=("arbitrary", "arbitrary"), vmem_limit_bytes=VMEM_LIMIT),
        name="mod_table",
    )(conds, ada_w, ada_b.reshape(DEPTH, 1, 6 * D_MODEL))


def _head_norm(t, g):
    ms = jnp.sum(t * t, axis=-1, keepdims=True) * (1.0 / QK_HEAD)
    return t * lax.rsqrt(ms + EPS) * g


def _value_rows(vt):
    ones = jnp.ones((ONES_ROWS, vt.shape[1]), BF16)
    parts = []
    for h in range(N_HEADS):
        parts += [vt[h * V_HEAD:(h + 1) * V_HEAD, :].astype(BF16), ones]
    return jnp.concatenate(parts, axis=0)


def _cache_keys_kernel(ckv_ref, kr_ref, wuk_ref, wuvt_ref, gk_ref, k_ref, vt_ref):
    ckv = ckv_ref[...].astype(BF16)
    kn = _dot(ckv, wuk_ref[...])
    vt_ref[...] = _value_rows(_dot_nt(wuvt_ref[...], ckv))
    kr = kr_ref[...]
    for h in range(N_HEADS):
        sl = slice(h * LANE, (h + 1) * LANE)
        k_ref[:, sl] = _head_norm(kn[:, sl] + kr, gk_ref[...]).astype(BF16)


def _cache_keys(cache_ckv, cache_kr_pad, wuk, wuvt, gk):
    nb, _, n, _ = cache_ckv.shape
    return pl.pallas_call(
        _cache_keys_kernel,
        grid=(DEPTH, nb),
        in_specs=[
            pl.BlockSpec((None, None, n, KV_RANK), lambda l, b: (b, l, 0, 0)),
            pl.BlockSpec((None, None, n, LANE), lambda l, b: (b, l, 0, 0)),
            pl.BlockSpec((None, KV_RANK, HEAD_PAD), lambda l, b: (l, 0, 0)),
            pl.BlockSpec((None, ATTN_WIDTH, KV_RANK), lambda l, b: (l, 0, 0)),
            pl.BlockSpec((None, 1, LANE), lambda l, b: (l, 0, 0)),
        ],
        out_specs=[
            pl.BlockSpec((None, None, n, HEAD_PAD), lambda l, b: (l, b, 0, 0)),
            pl.BlockSpec((None, None, VT_ROWS, n), lambda l, b: (l, b, 0, 0)),
        ],
        out_shape=[jax.ShapeDtypeStruct((DEPTH, nb, n, HEAD_PAD), BF16),
                   jax.ShapeDtypeStruct((DEPTH, nb, VT_ROWS, n), BF16)],
        compiler_params=pltpu.CompilerParams(
            dimension_semantics=("arbitrary", "arbitrary"), vmem_limit_bytes=VMEM_LIMIT),
        name="cache_keys",
    )(cache_ckv, cache_kr_pad, wuk, wuvt, gk)


def _lanes(col_ref, n):
    return jnp.concatenate([col_ref[...]] * (n // LANE), axis=1)


def _rms_t(t, g):
    return t * lax.rsqrt(jnp.mean(t * t, axis=0, keepdims=True) + EPS) * g


def _head_t(t, g, rope):
    ms = jnp.sum(t * t, axis=0, keepdims=True) * (1.0 / QK_HEAD)
    t = t * lax.rsqrt(ms + EPS) * g
    if rope is not None:
        cos, sin = rope
        half = QK_ROPE // 2
        r1 = t[QK_NOPE:QK_NOPE + half, :]
        r2 = t[QK_NOPE + half:QK_HEAD, :]
        t = jnp.concatenate([t[:QK_NOPE, :], r1 * cos - r2 * sin, r1 * sin + r2 * cos, t[QK_HEAD:, :]], axis=0)
    return t


def _inproj_kernel(*refs, latent):
    if latent:
        (_, x_ref, mod_ref, g1_ref, wu_ref, wrt_ref, qag_ref, wuqt_ref, kvg_ref, wukt_ref, wuvt_ref, gq_ref, gk_ref,
         rope_ref, u_ref, q_ref, k_ref, vt_ref) = refs
    else:
        (_, x_ref, mod_ref, g1_ref, wu_ref, wrt_ref, qag_ref, wuqt_ref, kvg_ref, wukt_ref, wuvt_ref, gq_ref, gk_ref,
         u_ref, q_ref, k_ref, vt_ref, ckv_ref, kr_ref) = refs
    n = x_ref.shape[0]
    shift1 = mod_ref[0:1, :]
    scale1 = mod_ref[1:2, :]
    h = (_rms(x_ref[...], g1_ref[...]) * (1.0 + scale1) + shift1).astype(BF16)
    u_ref[...] = _dot(h, wu_ref[...])
    pt = _dot_nt(wrt_ref[...], h)
    cq = _rms_t(pt[:Q_RANK, :], _lanes(qag_ref, n))
    qt = _dot(wuqt_ref[...], cq.astype(BF16))
    ckv = _rms_t(pt[Q_RANK:Q_RANK + KV_RANK, :], _lanes(kvg_ref, n))
    krt = pt[Q_RANK + KV_RANK:, :]
    ckv_b = ckv.astype(BF16)
    knt = _dot(wukt_ref[...], ckv_b)
    vt_ref[...] = _value_rows(_dot(wuvt_ref[...], ckv_b))
    rope = (rope_ref[0:QK_ROPE // 2, :], rope_ref[QK_ROPE // 2:QK_ROPE, :]) if latent else None
    gq = _lanes(gq_ref, n)
    gk = _lanes(gk_ref, n)
    zeros = jnp.zeros((LANE - QK_HEAD, n), F32)
    k_heads = []
    for hd in range(N_HEADS):
        qh = jnp.concatenate([qt[hd * QK_HEAD:(hd + 1) * QK_HEAD, :], zeros], axis=0)
        q_ref[hd * LANE:(hd + 1) * LANE, :] = _head_t(qh, gq, rope).astype(BF16)
        kh = jnp.concatenate([knt[hd * QK_NOPE:(hd + 1) * QK_NOPE, :], krt, zeros], axis=0)
        k_heads.append(_head_t(kh, gk, rope))
    k_ref[...] = jnp.concatenate(k_heads, axis=0).T.astype(BF16)
    if not latent:
        ckv_ref[...] = ckv.T
        kr_ref[...] = jnp.concatenate([jnp.zeros((QK_NOPE, n), F32), krt, zeros], axis=0).T


def _inproj(layer, x, mod, w, *, latent, seq_len, rope=None):
    t = x.shape[0]
    tb = min(PROJ_BLOCK, seq_len)
    bps = seq_len // tb
    rest = Q_RANK + KV_RANK + QK_ROPE
    in_specs = [
        pl.BlockSpec((tb, D_MODEL), lambda j, l: (j, 0)),
        _cond_spec(latent, bps),
        _layer_spec((1, D_MODEL)),
        _layer_spec((D_MODEL, POOL_WIDTH)),
        _layer_spec((rest, D_MODEL)),
        _layer_spec((Q_RANK, LANE)),
        _layer_spec((N_HEADS * QK_HEAD, Q_RANK)),
        _layer_spec((KV_RANK, LANE)),
        _layer_spec((N_HEADS * QK_NOPE, KV_RANK)),
        _layer_spec((ATTN_WIDTH, KV_RANK)),
        _layer_spec((LANE, LANE)),
        _layer_spec((LANE, LANE)),
    ]
    args = [x, mod, w['g1'], w['w_u'], w['w_rest_t'], w['qag_col'], w['w_uq_t'], w['kvg_col'], w['w_uk_t'],
            w['w_uvt'], w['gq_col'], w['gk_col']]
    tok = lambda width: pl.BlockSpec((tb, width), lambda j, l: (j, 0))
    vt_spec = pl.BlockSpec((None, VT_ROWS, tb), lambda j, l: (j // bps, 0, j % bps))
    out_specs = [tok(POOL_WIDTH), pl.BlockSpec((HEAD_PAD, tb), lambda j, l: (0, j)), tok(HEAD_PAD), vt_spec]
    out_shape = [jax.ShapeDtypeStruct((t, POOL_WIDTH), F32), jax.ShapeDtypeStruct((HEAD_PAD, t), BF16),
                 jax.ShapeDtypeStruct((t, HEAD_PAD), BF16), jax.ShapeDtypeStruct((t // seq_len, VT_ROWS, seq_len), BF16)]
    if latent:
        in_specs.append(pl.BlockSpec((QK_ROPE, tb), lambda j, l: (0, j % bps)))
        args.append(rope)
    else:
        out_specs += [tok(KV_RANK), tok(LANE)]
        out_shape += [jax.ShapeDtypeStruct((t, KV_RANK), F32), jax.ShapeDtypeStruct((t, LANE), F32)]
    return _call(functools.partial(_inproj_kernel, latent=latent), layer, (t // tb,), in_specs, out_specs, out_shape,
                 args, "inproj_latent" if latent else "inproj_context")


def _attn_kernel(*refs, n_seg):
    q_ref = refs[1]
    o_ref = refs[-1]
    tb = TOKEN_BLOCK
    tiles = [(s, c) for s in range(n_seg) for c in range(refs[2 + 2 * s].shape[0] // tb)]

    def fold(x):
        return x.reshape(x.shape[0] // 8, 8, x.shape[1])

    def score_tile(hd, s, c):
        sl = slice(hd * LANE, (hd + 1) * LANE)
        st = _dot(refs[2 + 2 * s][c * tb:(c + 1) * tb, sl], q_ref[sl, :])
        return st, jnp.max(fold(st), axis=0)

    def running_max(m8, mt):
        return mt if m8 is None else jnp.maximum(m8, mt)

    st_cur, m8 = [], None
    for s, c in tiles:
        st, mt = score_tile(0, s, c)
        st_cur.append(st)
        m8 = running_max(m8, mt)
    outs = []
    for hd in range(N_HEADS):
        vs = slice(hd * V_ROWS, (hd + 1) * V_ROWS)
        m = jnp.max(m8, axis=0, keepdims=True)
        st_next, m8 = [], None
        acc = None
        pending = None

        def pv_step(acc, pending):
            pb, s, c = pending
            pv = _dot(refs[3 + 2 * s][vs, c * tb:(c + 1) * tb], pb)
            return pv if acc is None else acc + pv

        for i, (s, c) in enumerate(tiles):
            if hd + 1 < N_HEADS:
                st, mt = score_tile(hd + 1, s, c)
                st_next.append(st)
                m8 = running_max(m8, mt)
            if pending is not None:
                acc = pv_step(acc, pending)
            pending = (jnp.exp2((st_cur[i] - m).astype(BF16)), s, c)
        acc = pv_step(acc, pending)
        outs.append(acc[:V_HEAD, :] * (1.0 / acc[V_HEAD:V_HEAD + 1, :]))
        st_cur = st_next
    o_ref[...] = jnp.concatenate(outs, axis=0).T.astype(BF16)


def _attention(layer, q, segs, *, seq_len):
    t = q.shape[1]
    tb = TOKEN_BLOCK
    bps = seq_len // tb
    in_specs = [pl.BlockSpec((HEAD_PAD, tb), lambda j, l: (0, j))]
    args = [q]
    for k, vt, per_layer in segs:
        n = k.shape[-2]
        if per_layer:
            in_specs += [pl.BlockSpec((None, None, n, HEAD_PAD), lambda j, l: (l[0], j // bps, 0, 0)),
                         pl.BlockSpec((None, None, VT_ROWS, n), lambda j, l: (l[0], j // bps, 0, 0))]
        else:
            in_specs += [pl.BlockSpec((None, n, HEAD_PAD), lambda j, l: (j // bps, 0, 0)),
                         pl.BlockSpec((None, VT_ROWS, n), lambda j, l: (j // bps, 0, 0))]
        args += [k, vt]
    return _call(functools.partial(_attn_kernel, n_seg=len(segs)), layer, (t // tb,), in_specs,
                 pl.BlockSpec((tb, ATTN_WIDTH), lambda j, l: (j, 0)), jax.ShapeDtypeStruct((t, ATTN_WIDTH), BF16),
                 args, "attention_%dseg" % len(segs))


def _route(logits_t, bias):
    scores = _sigmoid(logits_t)
    sel = scores + bias
    s_rows = [scores[e:e + 1, :] for e in range(N_EXPERTS)]
    r = [sel[e:e + 1, :] for e in range(N_EXPERTS)]
    picked = []
    group_score = []
    for g in range(N_GROUPS):
        members = range(g * EXPERTS_PER_GROUP, (g + 1) * EXPERTS_PER_GROUP)
        total = None
        for e in members:
            rank = None
            for j in members:
                if j == e:
                    continue
                ahead = (r[j] >= r[e]) if j < e else (r[j] > r[e])
                ahead = ahead.astype(F32)
                rank = ahead if rank is None else rank + ahead
            pick = rank < 2.0
            picked.append(pick)
            contrib = jnp.where(pick, r[e], 0.0)
            total = contrib if total is None else total + contrib
        group_score.append(total)
    gates = []
    for g in range(N_GROUPS):
        best = None
        for j in range(N_GROUPS):
            if j == g:
                continue
            wins = (group_score[g] > group_score[j]) if j < g else (group_score[g] >= group_score[j])
            best = wins if best is None else jnp.logical_and(best, wins)
        for e in range(g * EXPERTS_PER_GROUP, (g + 1) * EXPERTS_PER_GROUP):
            gates.append(jnp.where(jnp.logical_and(best, picked[e]), s_rows[e], 0.0))
    denom = functools.reduce(jnp.add, gates)
    inv = 1.0 / denom
    return [g * inv for g in gates]


def _post_kernel(_, x_ref, up_ref, uc_ref, un_ref, attn_ref, mod_ref, poolw_ref, pscale_ref, wout_ref, g2_ref,
                 rwt_ref, rb_ref, x1_ref, h2_ref, gates_ref, uext_ref, *, blocks_per_seq, seq_len):
    tb = x_ref.shape[0]
    halo = POOL_HALO
    jb = pl.program_id(0) % blocks_per_seq
    uext_ref[0:halo, :] = jnp.where(jb == 0, 0.0, up_ref[tb - halo:tb, :])
    uext_ref[halo:halo + tb, :] = uc_ref[...]
    uext_ref[halo + tb:, :] = jnp.where(jb == blocks_per_seq - 1, 0.0, un_ref[0:halo, :])
    pos = jb * tb + lax.broadcasted_iota(jnp.int32, (tb, 1), 0)
    pooled = []
    for g, w in enumerate(POOL_WINDOWS):
        half = w // 2
        cols = slice(g * POOL_GROUP, (g + 1) * POOL_GROUP)
        total = None
        for d in range(-half, half):
            part = uext_ref[halo + d:halo + d + tb, cols]
            total = part if total is None else total + part
        count = jnp.minimum(pos + half, seq_len) - jnp.maximum(pos - half, 0)
        diff = total / count.astype(F32) - uc_ref[:, cols]
        pooled.append(_dot(diff.astype(BF16), poolw_ref[g]))
    pool = jnp.concatenate(pooled, axis=-1) * pscale_ref[...]
    mixed = _dot(jnp.concatenate([pool.astype(BF16), attn_ref[...]], axis=-1), wout_ref[...])
    gate1 = mod_ref[2:3, :]
    shift2 = mod_ref[3:4, :]
    scale2 = mod_ref[4:5, :]
    x1 = x_ref[...] + gate1 * mixed
    x1_ref[...] = x1
    h2 = (_rms(x1, g2_ref[...]) * (1.0 + scale2) + shift2).astype(BF16)
    h2_ref[...] = h2
    gate_rows = _route(_dot_nt(rwt_ref[...], h2), rb_ref[...])
    zeros = jnp.zeros((LANE - EXPERT_CHUNK, tb), F32)
    pieces = []
    for c in range(N_EXPERTS // EXPERT_CHUNK):
        pieces += gate_rows[c * EXPERT_CHUNK:(c + 1) * EXPERT_CHUNK] + [zeros]
    gates_ref[...] = jnp.concatenate(pieces, axis=0).T


def _post(layer, x, u, attn, mod, w, router_wt, router_b, *, latent, seq_len):
    t = x.shape[0]
    tb = min(PROJ_BLOCK, seq_len)
    bps = seq_len // tb
    n_chunks = N_EXPERTS // EXPERT_CHUNK
    n_blocks = t // tb
    tok = lambda width: pl.BlockSpec((tb, width), lambda j, l: (j, 0))
    in_specs = [
        tok(D_MODEL),
        pl.BlockSpec((tb, POOL_WIDTH), lambda j, l: (jnp.maximum(j - 1, 0), 0)),
        tok(POOL_WIDTH),
        pl.BlockSpec((tb, POOL_WIDTH), lambda j, l: (jnp.minimum(j + 1, n_blocks - 1), 0)),
        tok(ATTN_WIDTH),
        _cond_spec(latent, bps),
        _layer_spec((len(POOL_WINDOWS), POOL_GROUP, POOL_GROUP)),
        _layer_spec((1, POOL_WIDTH)),
        _layer_spec((2 * ATTN_WIDTH, D_MODEL)),
        _layer_spec((1, D_MODEL)),
        pl.BlockSpec((N_EXPERTS, D_MODEL), lambda j, l: (0, 0)),
        pl.BlockSpec((N_EXPERTS, 1), lambda j, l: (0, 0)),
    ]
    out_shape = [
        jax.ShapeDtypeStruct((t, D_MODEL), F32),
        jax.ShapeDtypeStruct((t, D_MODEL), BF16),
        jax.ShapeDtypeStruct((t, n_chunks * LANE), F32),
    ]
    return _call(functools.partial(_post_kernel, blocks_per_seq=bps, seq_len=seq_len), layer, (n_blocks,), in_specs,
                 [tok(D_MODEL), tok(D_MODEL), tok(n_chunks * LANE)], out_shape,
                 [x, u, u, u, attn, mod, w['pool_w'], w['pool_scale'], w['w_out'], w['g2'], router_wt, router_b],
                 "post_latent" if latent else "post_context",
                 scratch_shapes=[pltpu.VMEM((tb + 2 * POOL_HALO, POOL_WIDTH), F32)])


def _moe_kernel(_, h2_ref, gates_ref, x1_ref, mod_ref, wg_ref, wu_ref, wd_ref, o_ref, acc_ref):
    c = pl.program_id(1)

    @pl.when(c == 0)
    def _():
        acc_ref[...] = jnp.zeros_like(acc_ref)

    h = h2_ref[...]
    total = None
    for e in range(EXPERT_CHUNK):
        a = _dot(h, wg_ref[e])
        b = _dot(h, wu_ref[e])
        act = a * _sigmoid(a) * b * gates_ref[:, e:e + 1]
        y = _dot(act.astype(BF16), wd_ref[e])
        total = y if total is None else total + y
    acc_ref[...] += total

    @pl.when(c == pl.num_programs(1) - 1)
    def _():
        o_ref[...] = x1_ref[...] + mod_ref[5:6, :] * acc_ref[...]


def _moe(layer, h2, gates, x1, mod, w, *, latent, seq_len):
    t = h2.shape[0]
    tm = MOE_BLOCK
    bps = seq_len // tm if latent else 1
    if latent:
        cond = pl.BlockSpec((None, None, 6, D_MODEL), lambda i, c, l: (l[0], 1 + i // bps, 0, 0))
    else:
        cond = pl.BlockSpec((None, None, 6, D_MODEL), lambda i, c, l: (l[0], 0, 0, 0))
    in_specs = [
        pl.BlockSpec((tm, D_MODEL), lambda i, c, l: (i, 0)),
        pl.BlockSpec((tm, LANE), lambda i, c, l: (i, c)),
        pl.BlockSpec((tm, D_MODEL), lambda i, c, l: (i, 0)),
        cond,
        pl.BlockSpec((None, EXPERT_CHUNK, D_MODEL, D_EXPERT), lambda i, c, l: (l[0], c, 0, 0)),
        pl.BlockSpec((None, EXPERT_CHUNK, D_MODEL, D_EXPERT), lambda i, c, l: (l[0], c, 0, 0)),
        pl.BlockSpec((None, EXPERT_CHUNK, D_EXPERT, D_MODEL), lambda i, c, l: (l[0], c, 0, 0)),
    ]
    return _call(_moe_kernel, layer, (t // tm, N_EXPERTS // EXPERT_CHUNK), in_specs,
                 pl.BlockSpec((tm, D_MODEL), lambda i, c, l: (i, 0)), jax.ShapeDtypeStruct((t, D_MODEL), F32),
                 [h2, gates, x1, mod, w['w_gate'], w['w_up'], w['w_down']],
                 "experts_latent" if latent else "experts_context",
                 scratch_shapes=[pltpu.VMEM((tm, D_MODEL), F32)])


def _pad_heads(w, width):
    lead = w.shape[:-1]
    w = w.reshape(lead + (N_HEADS, width))
    w = jnp.pad(w, [(0, 0)] * len(lead) + [(0, 0), (0, LANE - width)])
    return w.reshape(lead + (HEAD_PAD,))


def _pad_gain(g):
    return jnp.pad(g, ((0, 0), (0, LANE - QK_HEAD)))[:, None, :]


def _rope_table(rows):
    row = jnp.repeat(jnp.arange(rows), GRID_W).astype(F32)
    col = jnp.tile(jnp.arange(GRID_W), rows).astype(F32)
    n_freq = QK_ROPE // 4
    inv = ROPE_BASE ** (-jnp.arange(n_freq, dtype=F32) / n_freq)
    ang = jnp.concatenate([row[:, None] * inv, col[:, None] * inv], axis=-1)
    return jnp.concatenate([jnp.cos(ang), jnp.sin(ang)], axis=-1).T


def _col(g, rows):
    g = jnp.pad(g, ((0, 0), (0, rows - g.shape[1])))
    return jnp.broadcast_to(g[:, :, None], g.shape + (LANE,))


def kernel(x_prompt, x_sample, cache_ckv, cache_krope, c, c_ctx, ada_w, ada_b, norm1_g, norm2_g, w_in, pool_w,
           pool_scale, q_a_norm_g, w_uq, kv_a_norm_g, w_ukv, q_norm_g, k_norm_g, w_out, router_w, router_b,
           w_gate, w_up, w_down):
    batch, seq, _ = x_prompt.shape
    dec_batch, dec_seq, _ = x_sample.shape

    conds = jnp.concatenate([c_ctx[None, :], c, jnp.zeros((8 - 1 - dec_batch, D_MODEL), F32)], axis=0)
    mod = _mod_table(conds, ada_w, ada_b).reshape(DEPTH, 8, 6, D_MODEL)

    w_ukv_h = w_ukv.reshape(DEPTH, KV_RANK, N_HEADS, QK_NOPE + V_HEAD)
    w_uk = w_ukv_h[..., :QK_NOPE].reshape(DEPTH, KV_RANK, N_HEADS * QK_NOPE)
    q_scale = math.log2(math.e) / math.sqrt(QK_HEAD)
    weights = {
        'g1': norm1_g[:, None, :],
        'g2': norm2_g[:, None, :],
        'w_u': w_in[:, :, :POOL_WIDTH].astype(BF16),
        'w_rest_t': jnp.swapaxes(w_in[:, :, POOL_WIDTH:], 1, 2).astype(BF16),
        'qag_col': _col(q_a_norm_g, Q_RANK),
        'kvg_col': _col(kv_a_norm_g, KV_RANK),
        'w_uq_t': jnp.swapaxes(w_uq, 1, 2).astype(BF16),
        'w_uk_t': jnp.swapaxes(w_uk, 1, 2).astype(BF16),
        'w_uk': _pad_heads(w_uk, QK_NOPE).astype(BF16),
        'w_uvt': jnp.swapaxes(w_ukv_h[..., QK_NOPE:].reshape(DEPTH, KV_RANK, ATTN_WIDTH), 1, 2).astype(BF16),
        'gq_col': _col(q_norm_g * q_scale, LANE),
        'gk_col': _col(k_norm_g, LANE),
        'gk': _pad_gain(k_norm_g),
        'pool_w': pool_w.astype(BF16),
        'pool_scale': pool_scale[:, None, :],
        'w_out': w_out.astype(BF16),
        'w_gate': w_gate.astype(BF16),
        'w_up': w_up.astype(BF16),
        'w_down': w_down.astype(BF16),
    }
    router_wt = router_w.T.astype(BF16)
    router_bc = router_b[:, None]
    rope = _rope_table(dec_seq // GRID_W)

    cache_kr_pad = jnp.pad(cache_krope, ((0, 0), (0, 0), (0, 0), (QK_NOPE, LANE - QK_HEAD)))
    k_cache, vt_cache = _cache_keys(cache_ckv, cache_kr_pad, weights['w_uk'], weights['w_uvt'], weights['gk'])

    def layer_step(carry, layer):
        xc, xl = carry
        u, q, k, vt, ckv, kr = _inproj(layer, xc, mod, weights, latent=False, seq_len=seq)
        ctx_seg = (k.reshape(batch, seq, HEAD_PAD), vt, False)
        attn = _attention(layer, q, [ctx_seg], seq_len=seq)
        x1, h2, gates = _post(layer, xc, u, attn, mod, weights, router_wt, router_bc, latent=False, seq_len=seq)
        xc = _moe(layer, h2, gates, x1, mod, weights, latent=False, seq_len=seq)
        u, q, k, vt = _inproj(layer, xl, mod, weights, latent=True, seq_len=dec_seq, rope=rope)
        lat_seg = (k.reshape(dec_batch, dec_seq, HEAD_PAD), vt, False)
        attn = _attention(layer, q, [(k_cache, vt_cache, True), lat_seg], seq_len=dec_seq)
        x1, h2, gates = _post(layer, xl, u, attn, mod, weights, router_wt, router_bc, latent=True, seq_len=dec_seq)
        xl = _moe(layer, h2, gates, x1, mod, weights, latent=True, seq_len=dec_seq)
        return (xc, xl), (ckv, kr)

    layers = jnp.arange(DEPTH, dtype=jnp.int32).reshape(DEPTH, 1)
    (xc, xl), (ckv_all, kr_all) = lax.scan(
        layer_step, (x_prompt.reshape(batch * seq, D_MODEL), x_sample.reshape(dec_batch * dec_seq, D_MODEL)), layers,
        unroll=True)

    new_ckv = jnp.swapaxes(ckv_all.reshape(DEPTH, batch, seq, KV_RANK), 0, 1)
    new_krope = jnp.swapaxes(kr_all[:, :, QK_NOPE:QK_HEAD].reshape(DEPTH, batch, seq, QK_ROPE), 0, 1)
    return (xc.reshape(batch, seq, D_MODEL), xl.reshape(dec_batch, dec_seq, D_MODEL), new_ckv, new_krope)
```

```python
import functools
import math

import jax
import jax.numpy as jnp
from jax import lax
from jax.experimental import pallas as pl
from jax.experimental.pallas import tpu as pltpu

D_MODEL = 1024
DEPTH = 4
GRID_W = 64
POOL_WINDOWS = (2, 4, 8, 16)
POOL_GROUP = 128
POOL_WIDTH = POOL_GROUP * len(POOL_WINDOWS)
N_HEADS = 8
QK_NOPE = 64
QK_ROPE = 32
QK_HEAD = QK_NOPE + QK_ROPE
V_HEAD = 64
Q_RANK = 384
KV_RANK = 256
ATTN_WIDTH = N_HEADS * V_HEAD
N_EXPERTS = 16
N_GROUPS = 4
EXPERTS_PER_GROUP = N_EXPERTS // N_GROUPS
D_EXPERT = 256
ROPE_BASE = 10000.0
EPS = 1e-6

LANE = 128
HEAD_PAD = N_HEADS * LANE
ONES_ROWS = 16
V_ROWS = V_HEAD + ONES_ROWS
VT_ROWS = N_HEADS * V_ROWS
POOL_HALO = max(POOL_WINDOWS)
TOKEN_BLOCK = 256
EXP_LAG = 3
PV_LAG = 2
PROJ_BLOCK = 512
MOE_BLOCK = 1024
EXPERT_CHUNK = 4
VMEM_LIMIT = 56 * 1024 * 1024

BF16 = jnp.bfloat16
F32 = jnp.float32


def _dot(a, b):
    return jnp.dot(a, b, preferred_element_type=F32)


def _dot_nt(a, b):
    return lax.dot_general(a, b, (((1,), (1,)), ((), ())), preferred_element_type=F32)


def _rms(x, g):
    return x * lax.rsqrt(jnp.mean(x * x, axis=-1, keepdims=True) + EPS) * g


def _sigmoid(x):
    return 1.0 / (1.0 + jnp.exp(-x))


def _call(body, layer, grid, in_specs, out_specs, out_shape, args, name, scratch_shapes=(), flags=None):
    return pl.pallas_call(
        body,
        grid_spec=pltpu.PrefetchScalarGridSpec(
            num_scalar_prefetch=1, grid=grid, in_specs=in_specs, out_specs=out_specs, scratch_shapes=scratch_shapes),
        out_shape=out_shape,
        compiler_params=pltpu.CompilerParams(
            dimension_semantics=("arbitrary",) * len(grid), vmem_limit_bytes=VMEM_LIMIT, flags=flags),
        name=name,
    )(layer, *args)


def _layer_spec(shape):
    return pl.BlockSpec((None,) + shape, lambda *a: (a[-1][0],) + (0,) * len(shape))


def _cond_spec(latent, blocks_per_seq):
    if latent:
        return pl.BlockSpec((None, None, 6, D_MODEL), lambda j, *a: (a[-1][0], 1 + j // blocks_per_seq, 0, 0))
    return pl.BlockSpec((None, None, 6, D_MODEL), lambda j, *a: (a[-1][0], 0, 0, 0))


def _mod_kernel(cond_ref, w_ref, b_ref, o_ref):
    cond = cond_ref[...]
    s = (cond * _sigmoid(cond)).astype(BF16)
    o_ref[...] = _dot(s, w_ref[...].astype(BF16)) + b_ref[...]


def _mod_table(conds, ada_w, ada_b):
    n_rows = conds.shape[0]
    tn = 1536
    return pl.pallas_call(
        _mod_kernel,
        grid=(DEPTH, 6 * D_MODEL // tn),
        in_specs=[
            pl.BlockSpec((n_rows, D_MODEL), lambda l, n: (0, 0)),
            pl.BlockSpec((None, D_MODEL, tn), lambda l, n: (l, 0, n)),
            pl.BlockSpec((None, 1, tn), lambda l, n: (l, 0, n)),
        ],
        out_specs=pl.BlockSpec((None, n_rows, tn), lambda l, n: (l, 0, n)),
        out_shape=jax.ShapeDtypeStruct((DEPTH, n_rows, 6 * D_MODEL), F32),
        compiler_params=pltpu.CompilerParams(
            dimension_semantics=("arbitrary", "arbitrary"), vmem_limit_bytes=VMEM_LIMIT),
        name="mod_table",
    )(conds, ada_w, ada_b.reshape(DEPTH, 1, 6 * D_MODEL))


def _head_norm(t, g):
    ms = jnp.sum(t * t, axis=-1, keepdims=True) * (1.0 / QK_HEAD)
    return t * lax.rsqrt(ms + EPS) * g


def _value_rows(vt):
    ones = jnp.ones((ONES_ROWS, vt.shape[1]), BF16)
    parts = []
    for h in range(N_HEADS):
        parts += [vt[h * V_HEAD:(h + 1) * V_HEAD, :].astype(BF16), ones]
    return jnp.concatenate(parts, axis=0)


def _cache_keys_kernel(ckv_ref, kr_ref, wuk_ref, wuvt_ref, gk_ref, k_ref, vt_ref):
    ckv = ckv_ref[...].astype(BF16)
    kn = _dot(ckv, wuk_ref[...])
    vt_ref[...] = _value_rows(_dot_nt(wuvt_ref[...], ckv))
    kr = kr_ref[...]
    for h in range(N_HEADS):
        sl = slice(h * LANE, (h + 1) * LANE)
        k_ref[:, sl] = _head_norm(kn[:, sl] + kr, gk_ref[...]).astype(BF16)


def _cache_keys(cache_ckv, cache_kr_pad, wuk, wuvt, gk):
    nb, _, n, _ = cache_ckv.shape
    return pl.pallas_call(
        _cache_keys_kernel,
        grid=(DEPTH, nb),
        in_specs=[
            pl.BlockSpec((None, None, n, KV_RANK), lambda l, b: (b, l, 0, 0)),
            pl.BlockSpec((None, None, n, LANE), lambda l, b: (b, l, 0, 0)),
            pl.BlockSpec((None, KV_RANK, HEAD_PAD), lambda l, b: (l, 0, 0)),
            pl.BlockSpec((None, ATTN_WIDTH, KV_RANK), lambda l, b: (l, 0, 0)),
            pl.BlockSpec((None, 1, LANE), lambda l, b: (l, 0, 0)),
        ],
        out_specs=[
            pl.BlockSpec((None, None, n, HEAD_PAD), lambda l, b: (l, b, 0, 0)),
            pl.BlockSpec((None, None, VT_ROWS, n), lambda l, b: (l, b, 0, 0)),
        ],
        out_shape=[jax.ShapeDtypeStruct((DEPTH, nb, n, HEAD_PAD), BF16),
                   jax.ShapeDtypeStruct((DEPTH, nb, VT_ROWS, n), BF16)],
        compiler_params=pltpu.CompilerParams(
            dimension_semantics=("arbitrary", "arbitrary"), vmem_limit_bytes=VMEM_LIMIT),
        name="cache_keys",
    )(cache_ckv, cache_kr_pad, wuk, wuvt, gk)


def _lanes(col_ref, n):
    return jnp.concatenate([col_ref[...]] * (n // LANE), axis=1)


def _rms_t(t, g):
    return t * lax.rsqrt(jnp.mean(t * t, axis=0, keepdims=True) + EPS) * g


def _head_t(t, g, rope):
    ms = jnp.sum(t * t, axis=0, keepdims=True) * (1.0 / QK_HEAD)
    t = t * lax.rsqrt(ms + EPS) * g
    if rope is not None:
        cos, sin = rope
        half = QK_ROPE // 2
        r1 = t[QK_NOPE:QK_NOPE + half, :]
        r2 = t[QK_NOPE + half:QK_HEAD, :]
        t = jnp.concatenate([t[:QK_NOPE, :], r1 * cos - r2 * sin, r1 * sin + r2 * cos, t[QK_HEAD:, :]], axis=0)
    return t


def _inproj_kernel(*refs, latent):
    if latent:
        (_, x_ref, mod_ref, g1_ref, wu_ref, wrt_ref, qag_ref, wuqt_ref, kvg_ref, wukt_ref, wuvt_ref, gq_ref, gk_ref,
         rope_ref, u_ref, q_ref, k_ref, vt_ref) = refs
    else:
        (_, x_ref, mod_ref, g1_ref, wu_ref, wrt_ref, qag_ref, wuqt_ref, kvg_ref, wukt_ref, wuvt_ref, gq_ref, gk_ref,
         u_ref, q_ref, k_ref, vt_ref, ckv_ref, kr_ref) = refs
    n = x_ref.shape[0]
    shift1 = mod_ref[0:1, :]
    scale1 = mod_ref[1:2, :]
    h = (_rms(x_ref[...], g1_ref[...]) * (1.0 + scale1) + shift1).astype(BF16)
    u_ref[...] = _dot(h, wu_ref[...])
    pt = _dot_nt(wrt_ref[...], h)
    cq = _rms_t(pt[:Q_RANK, :], _lanes(qag_ref, n))
    qt = _dot(wuqt_ref[...], cq.astype(BF16))
    ckv = _rms_t(pt[Q_RANK:Q_RANK + KV_RANK, :], _lanes(kvg_ref, n))
    krt = pt[Q_RANK + KV_RANK:, :]
    ckv_b = ckv.astype(BF16)
    knt = _dot(wukt_ref[...], ckv_b)
    vt_ref[...] = _value_rows(_dot(wuvt_ref[...], ckv_b))
    rope = (rope_ref[0:QK_ROPE // 2, :], rope_ref[QK_ROPE // 2:QK_ROPE, :]) if latent else None
    gq = _lanes(gq_ref, n)
    gk = _lanes(gk_ref, n)
    zeros = jnp.zeros((LANE - QK_HEAD, n), F32)
    k_heads = []
    for hd in range(N_HEADS):
        qh = jnp.concatenate([qt[hd * QK_HEAD:(hd + 1) * QK_HEAD, :], zeros], axis=0)
        q_ref[hd * LANE:(hd + 1) * LANE, :] = _head_t(qh, gq, rope).astype(BF16)
        kh = jnp.concatenate([knt[hd * QK_NOPE:(hd + 1) * QK_NOPE, :], krt, zeros], axis=0)
        k_heads.append(_head_t(kh, gk, rope))
    k_ref[...] = jnp.concatenate(k_heads, axis=0).T.astype(BF16)
    if not latent:
        ckv_ref[...] = ckv.T
        kr_ref[...] = jnp.concatenate([jnp.zeros((QK_NOPE, n), F32), krt, zeros], axis=0).T


def _inproj(layer, x, mod, w, *, latent, seq_len, rope=None):
    t = x.shape[0]
    tb = min(PROJ_BLOCK, seq_len)
    bps = seq_len // tb
    rest = Q_RANK + KV_RANK + QK_ROPE
    in_specs = [
        pl.BlockSpec((tb, D_MODEL), lambda j, l: (j, 0)),
        _cond_spec(latent, bps),
        _layer_spec((1, D_MODEL)),
        _layer_spec((D_MODEL, POOL_WIDTH)),
        _layer_spec((rest, D_MODEL)),
        _layer_spec((Q_RANK, LANE)),
        _layer_spec((N_HEADS * QK_HEAD, Q_RANK)),
        _layer_spec((KV_RANK, LANE)),
        _layer_spec((N_HEADS * QK_NOPE, KV_RANK)),
        _layer_spec((ATTN_WIDTH, KV_RANK)),
        _layer_spec((LANE, LANE)),
        _layer_spec((LANE, LANE)),
    ]
    args = [x, mod, w['g1'], w['w_u'], w['w_rest_t'], w['qag_col'], w['w_uq_t'], w['kvg_col'], w['w_uk_t'],
            w['w_uvt'], w['gq_col'], w['gk_col']]
    tok = lambda width: pl.BlockSpec((tb, width), lambda j, l: (j, 0))
    vt_spec = pl.BlockSpec((None, VT_ROWS, tb), lambda j, l: (j // bps, 0, j % bps))
    out_specs = [tok(POOL_WIDTH), pl.BlockSpec((HEAD_PAD, tb), lambda j, l: (0, j)), tok(HEAD_PAD), vt_spec]
    out_shape = [jax.ShapeDtypeStruct((t, POOL_WIDTH), F32), jax.ShapeDtypeStruct((HEAD_PAD, t), BF16),
                 jax.ShapeDtypeStruct((t, HEAD_PAD), BF16), jax.ShapeDtypeStruct((t // seq_len, VT_ROWS, seq_len), BF16)]
    if latent:
        in_specs.append(pl.BlockSpec((QK_ROPE, tb), lambda j, l: (0, j % bps)))
        args.append(rope)
    else:
        out_specs += [tok(KV_RANK), tok(LANE)]
        out_shape += [jax.ShapeDtypeStruct((t, KV_RANK), F32), jax.ShapeDtypeStruct((t, LANE), F32)]
    return _call(functools.partial(_inproj_kernel, latent=latent), layer, (t // tb,), in_specs, out_specs, out_shape,
                 args, "inproj_latent" if latent else "inproj_context")


def _attn_kernel(*refs, n_seg):
    q_ref = refs[1]
    o_ref = refs[-1]
    tb = TOKEN_BLOCK
    tiles = [(s, c) for s in range(n_seg) for c in range(refs[2 + 2 * s].shape[0] // tb)]

    def fold(x):
        return x.reshape(x.shape[0] // 8, 8, x.shape[1])

    def score_tile(hd, s, c):
        sl = slice(hd * LANE, (hd + 1) * LANE)
        st = _dot(refs[2 + 2 * s][c * tb:(c + 1) * tb, sl], q_ref[sl, :])
        return st, jnp.max(fold(st), axis=0)

    def running_max(m8, mt):
        return mt if m8 is None else jnp.maximum(m8, mt)

    st_cur, m8 = [], None
    for s, c in tiles:
        st, mt = score_tile(0, s, c)
        st_cur.append(st)
        m8 = running_max(m8, mt)
    outs = []
    for hd in range(N_HEADS):
        vs = slice(hd * V_ROWS, (hd + 1) * V_ROWS)
        m = jnp.max(m8, axis=0, keepdims=True)
        st_next, m8 = [], None
        acc = None
        pending = []
        probes = []

        def pv_step(acc, item):
            pb, s, c = item
            pv = _dot(refs[3 + 2 * s][vs, c * tb:(c + 1) * tb], pb)
            return pv if acc is None else acc + pv

        for i, (s, c) in enumerate(tiles):
            m_i = m
            if hd + 1 < N_HEADS:
                st, mt = score_tile(hd + 1, s, c)
                st_next.append(st)
                m8 = running_max(m8, mt)
                probes.append(mt[0:1, :])
                if i >= EXP_LAG:
                    probe = probes[i - EXP_LAG]
                    m_i = jnp.where(probe != probe, probe, m)
            if len(pending) >= PV_LAG:
                acc = pv_step(acc, pending.pop(0))
            pending.append((jnp.exp2((st_cur[i] - m_i).astype(BF16)), s, c))
        for item in pending:
            acc = pv_step(acc, item)
        outs.append(acc[:V_HEAD, :] * (1.0 / acc[V_HEAD:V_HEAD + 1, :]))
        st_cur = st_next
    o_ref[...] = jnp.concatenate(outs, axis=0).T.astype(BF16)


def _attention(layer, q, segs, *, seq_len):
    t = q.shape[1]
    tb = TOKEN_BLOCK
    bps = seq_len // tb
    in_specs = [pl.BlockSpec((HEAD_PAD, tb), lambda j, l: (0, j))]
    args = [q]
    for k, vt, per_layer in segs:
        n = k.shape[-2]
        if per_layer:
            in_specs += [pl.BlockSpec((None, None, n, HEAD_PAD), lambda j, l: (l[0], j // bps, 0, 0)),
                         pl.BlockSpec((None, None, VT_ROWS, n), lambda j, l: (l[0], j // bps, 0, 0))]
        else:
            in_specs += [pl.BlockSpec((None, n, HEAD_PAD), lambda j, l: (j // bps, 0, 0)),
                         pl.BlockSpec((None, VT_ROWS, n), lambda j, l: (j // bps, 0, 0))]
        args += [k, vt]
    return _call(functools.partial(_attn_kernel, n_seg=len(segs)), layer, (t // tb,), in_specs,
                 pl.BlockSpec((tb, ATTN_WIDTH), lambda j, l: (j, 0)), jax.ShapeDtypeStruct((t, ATTN_WIDTH), BF16),
                 args, "attention_%dseg" % len(segs))


def _route(logits_t, bias):
    scores = _sigmoid(logits_t)
    sel = scores + bias
    s_rows = [scores[e:e + 1, :] for e in range(N_EXPERTS)]
    r = [sel[e:e + 1, :] for e in range(N_EXPERTS)]
    picked = []
    group_score = []
    for g in range(N_GROUPS):
        members = range(g * EXPERTS_PER_GROUP, (g + 1) * EXPERTS_PER_GROUP)
        total = None
        for e in members:
            rank = None
            for j in members:
                if j == e:
                    continue
                ahead = (r[j] >= r[e]) if j < e else (r[j] > r[e])
                ahead = ahead.astype(F32)
                rank = ahead if rank is None else rank + ahead
            pick = rank < 2.0
            picked.append(pick)
            contrib = jnp.where(pick, r[e], 0.0)
            total = contrib if total is None else total + contrib
        group_score.append(total)
    gates = []
    for g in range(N_GROUPS):
        best = None
        for j in range(N_GROUPS):
            if j == g:
                continue
            wins = (group_score[g] > group_score[j]) if j < g else (group_score[g] >= group_score[j])
            best = wins if best is None else jnp.logical_and(best, wins)
        for e in range(g * EXPERTS_PER_GROUP, (g + 1) * EXPERTS_PER_GROUP):
            gates.append(jnp.where(jnp.logical_and(best, picked[e]), s_rows[e], 0.0))
    denom = functools.reduce(jnp.add, gates)
    inv = 1.0 / denom
    return [g * inv for g in gates]


def _post_kernel(_, x_ref, up_ref, uc_ref, un_ref, attn_ref, mod_ref, poolw_ref, pscale_ref, wout_ref, g2_ref,
                 rwt_ref, rb_ref, x1_ref, h2_ref, gates_ref, uext_ref, *, blocks_per_seq, seq_len):
    tb = x_ref.shape[0]
    halo = POOL_HALO
    jb = pl.program_id(0) % blocks_per_seq
    uext_ref[0:halo, :] = jnp.where(jb == 0, 0.0, up_ref[tb - halo:tb, :])
    uext_ref[halo:halo + tb, :] = uc_ref[...]
    uext_ref[halo + tb:, :] = jnp.where(jb == blocks_per_seq - 1, 0.0, un_ref[0:halo, :])
    pos = jb * tb + lax.broadcasted_iota(jnp.int32, (tb, 1), 0)
    pooled = []
    for g, w in enumerate(POOL_WINDOWS):
        half = w // 2
        cols = slice(g * POOL_GROUP, (g + 1) * POOL_GROUP)
        total = None
        for d in range(-half, half):
            part = uext_ref[halo + d:halo + d + tb, cols]
            total = part if total is None else total + part
        count = jnp.minimum(pos + half, seq_len) - jnp.maximum(pos - half, 0)
        diff = total / count.astype(F32) - uc_ref[:, cols]
        pooled.append(_dot(diff.astype(BF16), poolw_ref[g]))
    pool = jnp.concatenate(pooled, axis=-1) * pscale_ref[...]
    mixed = _dot(jnp.concatenate([pool.astype(BF16), attn_ref[...]], axis=-1), wout_ref[...])
    gate1 = mod_ref[2:3, :]
    shift2 = mod_ref[3:4, :]
    scale2 = mod_ref[4:5, :]
    x1 = x_ref[...] + gate1 * mixed
    x1_ref[...] = x1
    h2 = (_rms(x1, g2_ref[...]) * (1.0 + scale2) + shift2).astype(BF16)
    h2_ref[...] = h2
    gate_rows = _route(_dot_nt(rwt_ref[...], h2), rb_ref[...])
    zeros = jnp.zeros((LANE - EXPERT_CHUNK, tb), F32)
    pieces = []
    for c in range(N_EXPERTS // EXPERT_CHUNK):
        pieces += gate_rows[c * EXPERT_CHUNK:(c + 1) * EXPERT_CHUNK] + [zeros]
    gates_ref[...] = jnp.concatenate(pieces, axis=0).T


def _post(layer, x, u, attn, mod, w, router_wt, router_b, *, latent, seq_len):
    t = x.shape[0]
    tb = min(PROJ_BLOCK, seq_len)
    bps = seq_len // tb
    n_chunks = N_EXPERTS // EXPERT_CHUNK
    n_blocks = t // tb
    tok = lambda width: pl.BlockSpec((tb, width), lambda j, l: (j, 0))
    in_specs = [
        tok(D_MODEL),
        pl.BlockSpec((tb, POOL_WIDTH), lambda j, l: (jnp.maximum(j - 1, 0), 0)),
        tok(POOL_WIDTH),
        pl.BlockSpec((tb, POOL_WIDTH), lambda j, l: (jnp.minimum(j + 1, n_blocks - 1), 0)),
        tok(ATTN_WIDTH),
        _cond_spec(latent, bps),
        _layer_spec((len(POOL_WINDOWS), POOL_GROUP, POOL_GROUP)),
        _layer_spec((1, POOL_WIDTH)),
        _layer_spec((2 * ATTN_WIDTH, D_MODEL)),
        _layer_spec((1, D_MODEL)),
        pl.BlockSpec((N_EXPERTS, D_MODEL), lambda j, l: (0, 0)),
        pl.BlockSpec((N_EXPERTS, 1), lambda j, l: (0, 0)),
    ]
    out_shape = [
        jax.ShapeDtypeStruct((t, D_MODEL), F32),
        jax.ShapeDtypeStruct((t, D_MODEL), BF16),
        jax.ShapeDtypeStruct((t, n_chunks * LANE), F32),
    ]
    return _call(functools.partial(_post_kernel, blocks_per_seq=bps, seq_len=seq_len), layer, (n_blocks,), in_specs,
                 [tok(D_MODEL), tok(D_MODEL), tok(n_chunks * LANE)], out_shape,
                 [x, u, u, u, attn, mod, w['pool_w'], w['pool_scale'], w['w_out'], w['g2'], router_wt, router_b],
                 "post_latent" if latent else "post_context",
                 scratch_shapes=[pltpu.VMEM((tb + 2 * POOL_HALO, POOL_WIDTH), F32)])


def _moe_kernel(_, h2_ref, gates_ref, x1_ref, mod_ref, wg_ref, wu_ref, wd_ref, o_ref, acc_ref):
    c = pl.program_id(1)

    @pl.when(c == 0)
    def _():
        acc_ref[...] = jnp.zeros_like(acc_ref)

    h = h2_ref[...]
    total = None
    for e in range(EXPERT_CHUNK):
        a = _dot(h, wg_ref[e])
        b = _dot(h, wu_ref[e])
        act = a * _sigmoid(a) * b * gates_ref[:, e:e + 1]
        y = _dot(act.astype(BF16), wd_ref[e])
        total = y if total is None else total + y
    acc_ref[...] += total

    @pl.when(c == pl.num_programs(1) - 1)
    def _():
        o_ref[...] = x1_ref[...] + mod_ref[5:6, :] * acc_ref[...]


def _moe(layer, h2, gates, x1, mod, w, *, latent, seq_len):
    t = h2.shape[0]
    tm = MOE_BLOCK
    bps = seq_len // tm if latent else 1
    if latent:
        cond = pl.BlockSpec((None, None, 6, D_MODEL), lambda i, c, l: (l[0], 1 + i // bps, 0, 0))
    else:
        cond = pl.BlockSpec((None, None, 6, D_MODEL), lambda i, c, l: (l[0], 0, 0, 0))
    in_specs = [
        pl.BlockSpec((tm, D_MODEL), lambda i, c, l: (i, 0)),
        pl.BlockSpec((tm, LANE), lambda i, c, l: (i, c)),
        pl.BlockSpec((tm, D_MODEL), lambda i, c, l: (i, 0)),
        cond,
        pl.BlockSpec((None, EXPERT_CHUNK, D_MODEL, D_EXPERT), lambda i, c, l: (l[0], c, 0, 0)),
        pl.BlockSpec((None, EXPERT_CHUNK, D_MODEL, D_EXPERT), lambda i, c, l: (l[0], c, 0, 0)),
        pl.BlockSpec((None, EXPERT_CHUNK, D_EXPERT, D_MODEL), lambda i, c, l: (l[0], c, 0, 0)),
    ]
    return _call(_moe_kernel, layer, (t // tm, N_EXPERTS // EXPERT_CHUNK), in_specs,
                 pl.BlockSpec((tm, D_MODEL), lambda i, c, l: (i, 0)), jax.ShapeDtypeStruct((t, D_MODEL), F32),
                 [h2, gates, x1, mod, w['w_gate'], w['w_up'], w['w_down']],
                 "experts_latent" if latent else "experts_context",
                 scratch_shapes=[pltpu.VMEM((tm, D_MODEL), F32)])


def _pad_heads(w, width):
    lead = w.shape[:-1]
    w = w.reshape(lead + (N_HEADS, width))
    w = jnp.pad(w, [(0, 0)] * len(lead) + [(0, 0), (0, LANE - width)])
    return w.reshape(lead + (HEAD_PAD,))


def _pad_gain(g):
    return jnp.pad(g, ((0, 0), (0, LANE - QK_HEAD)))[:, None, :]


def _rope_table(rows):
    row = jnp.repeat(jnp.arange(rows), GRID_W).astype(F32)
    col = jnp.tile(jnp.arange(GRID_W), rows).astype(F32)
    n_freq = QK_ROPE // 4
    inv = ROPE_BASE ** (-jnp.arange(n_freq, dtype=F32) / n_freq)
    ang = jnp.concatenate([row[:, None] * inv, col[:, None] * inv], axis=-1)
    return jnp.concatenate([jnp.cos(ang), jnp.sin(ang)], axis=-1).T


def _col(g, rows):
    g = jnp.pad(g, ((0, 0), (0, rows - g.shape[1])))
    return jnp.broadcast_to(g[:, :, None], g.shape + (LANE,))


def kernel(x_prompt, x_sample, cache_ckv, cache_krope, c, c_ctx, ada_w, ada_b, norm1_g, norm2_g, w_in, pool_w,
           pool_scale, q_a_norm_g, w_uq, kv_a_norm_g, w_ukv, q_norm_g, k_norm_g, w_out, router_w, router_b,
           w_gate, w_up, w_down):
    batch, seq, _ = x_prompt.shape
    dec_batch, dec_seq, _ = x_sample.shape

    conds = jnp.concatenate([c_ctx[None, :], c, jnp.zeros((8 - 1 - dec_batch, D_MODEL), F32)], axis=0)
    mod = _mod_table(conds, ada_w, ada_b).reshape(DEPTH, 8, 6, D_MODEL)

    w_ukv_h = w_ukv.reshape(DEPTH, KV_RANK, N_HEADS, QK_NOPE + V_HEAD)
    w_uk = w_ukv_h[..., :QK_NOPE].reshape(DEPTH, KV_RANK, N_HEADS * QK_NOPE)
    q_scale = math.log2(math.e) / math.sqrt(QK_HEAD)
    weights = {
        'g1': norm1_g[:, None, :],
        'g2': norm2_g[:, None, :],
        'w_u': w_in[:, :, :POOL_WIDTH].astype(BF16),
        'w_rest_t': jnp.swapaxes(w_in[:, :, POOL_WIDTH:], 1, 2).astype(BF16),
        'qag_col': _col(q_a_norm_g, Q_RANK),
        'kvg_col': _col(kv_a_norm_g, KV_RANK),
        'w_uq_t': jnp.swapaxes(w_uq, 1, 2).astype(BF16),
        'w_uk_t': jnp.swapaxes(w_uk, 1, 2).astype(BF16),
        'w_uk': _pad_heads(w_uk, QK_NOPE).astype(BF16),
        'w_uvt': jnp.swapaxes(w_ukv_h[..., QK_NOPE:].reshape(DEPTH, KV_RANK, ATTN_WIDTH), 1, 2).astype(BF16),
        'gq_col': _col(q_norm_g * q_scale, LANE),
        'gk_col': _col(k_norm_g, LANE),
        'gk': _pad_gain(k_norm_g),
        'pool_w': pool_w.astype(BF16),
        'pool_scale': pool_scale[:, None, :],
        'w_out': w_out.astype(BF16),
        'w_gate': w_gate.astype(BF16),
        'w_up': w_up.astype(BF16),
        'w_down': w_down.astype(BF16),
    }
    router_wt = router_w.T.astype(BF16)
    router_bc = router_b[:, None]
    rope = _rope_table(dec_seq // GRID_W)

    cache_kr_pad = jnp.pad(cache_krope, ((0, 0), (0, 0), (0, 0), (QK_NOPE, LANE - QK_HEAD)))
    k_cache, vt_cache = _cache_keys(cache_ckv, cache_kr_pad, weights['w_uk'], weights['w_uvt'], weights['gk'])

    def layer_step(carry, layer):
        xc, xl = carry
        u, q, k, vt, ckv, kr = _inproj(layer, xc, mod, weights, latent=False, seq_len=seq)
        ctx_seg = (k.reshape(batch, seq, HEAD_PAD), vt, False)
        attn = _attention(layer, q, [ctx_seg], seq_len=seq)
        x1, h2, gates = _post(layer, xc, u, attn, mod, weights, router_wt, router_bc, latent=False, seq_len=seq)
        xc = _moe(layer, h2, gates, x1, mod, weights, latent=False, seq_len=seq)
        u, q, k, vt = _inproj(layer, xl, mod, weights, latent=True, seq_len=dec_seq, rope=rope)
        lat_seg = (k.reshape(dec_batch, dec_seq, HEAD_PAD), vt, False)
        attn = _attention(layer, q, [(k_cache, vt_cache, True), lat_seg], seq_len=dec_seq)
        x1, h2, gates = _post(layer, xl, u, attn, mod, weights, router_wt, router_bc, latent=True, seq_len=dec_seq)
        xl = _moe(layer, h2, gates, x1, mod, weights, latent=True, seq_len=dec_seq)
        return (xc, xl), (ckv, kr)

    layers = jnp.arange(DEPTH, dtype=jnp.int32).reshape(DEPTH, 1)
    (xc, xl), (ckv_all, kr_all) = lax.scan(
        layer_step, (x_prompt.reshape(batch * seq, D_MODEL), x_sample.reshape(dec_batch * dec_seq, D_MODEL)), layers,
        unroll=True)

    new_ckv = jnp.swapaxes(ckv_all.reshape(DEPTH, batch, seq, KV_RANK), 0, 1)
    new_krope = jnp.swapaxes(kr_all[:, :, QK_NOPE:QK_HEAD].reshape(DEPTH, batch, seq, QK_ROPE), 0, 1)
    return (xc.reshape(batch, seq, D_MODEL), xl.reshape(dec_batch, dec_seq, D_MODEL), new_ckv, new_krope)
```

```python
import functools
import math

import jax
import jax.numpy as jnp
from jax import lax
from jax.experimental import pallas as pl
from jax.experimental.pallas import tpu as pltpu

D_MODEL = 1024
DEPTH = 4
GRID_W = 64
POOL_WINDOWS = (2, 4, 8, 16)
POOL_GROUP = 128
POOL_WIDTH = POOL_GROUP * len(POOL_WINDOWS)
N_HEADS = 8
QK_NOPE = 64
QK_ROPE = 32
QK_HEAD = QK_NOPE + QK_ROPE
V_HEAD = 64
Q_RANK = 384
KV_RANK = 256
ATTN_WIDTH = N_HEADS * V_HEAD
N_EXPERTS = 16
N_GROUPS = 4
EXPERTS_PER_GROUP = N_EXPERTS // N_GROUPS
D_EXPERT = 256
ROPE_BASE = 10000.0
EPS = 1e-6

LANE = 128
HEAD_PAD = N_HEADS * LANE
ONES_ROWS = 16
V_ROWS = V_HEAD + ONES_ROWS
VT_ROWS = N_HEADS * V_ROWS
POOL_HALO = max(POOL_WINDOWS)
TOKEN_BLOCK = 256
EXP_LAG = 3
PV_LAG = 2
PROJ_BLOCK = 512
N_SUB = 2
MOE_BLOCK = 1024
EXPERT_CHUNK = 4
VMEM_LIMIT = 56 * 1024 * 1024

BF16 = jnp.bfloat16
F32 = jnp.float32


def _dot(a, b):
    return jnp.dot(a, b, preferred_element_type=F32)


def _dot_nt(a, b):
    return lax.dot_general(a, b, (((1,), (1,)), ((), ())), preferred_element_type=F32)


def _rms(x, g):
    return x * lax.rsqrt(jnp.mean(x * x, axis=-1, keepdims=True) + EPS) * g


def _sigmoid(x):
    return 1.0 / (1.0 + jnp.exp(-x))


def _call(body, layer, grid, in_specs, out_specs, out_shape, args, name, scratch_shapes=(), flags=None):
    return pl.pallas_call(
        body,
        grid_spec=pltpu.PrefetchScalarGridSpec(
            num_scalar_prefetch=1, grid=grid, in_specs=in_specs, out_specs=out_specs, scratch_shapes=scratch_shapes),
        out_shape=out_shape,
        compiler_params=pltpu.CompilerParams(
            dimension_semantics=("arbitrary",) * len(grid), vmem_limit_bytes=VMEM_LIMIT, flags=flags),
        name=name,
    )(layer, *args)


def _layer_spec(shape):
    return pl.BlockSpec((None,) + shape, lambda *a: (a[-1][0],) + (0,) * len(shape))


def _cond_spec(latent, blocks_per_seq):
    if latent:
        return pl.BlockSpec((None, None, 6, D_MODEL), lambda j, *a: (a[-1][0], 1 + j // blocks_per_seq, 0, 0))
    return pl.BlockSpec((None, None, 6, D_MODEL), lambda j, *a: (a[-1][0], 0, 0, 0))


def _mod_kernel(cond_ref, w_ref, b_ref, o_ref):
    cond = cond_ref[...]
    s = (cond * _sigmoid(cond)).astype(BF16)
    o_ref[...] = _dot(s, w_ref[...].astype(BF16)) + b_ref[...]


def _mod_table(conds, ada_w, ada_b):
    n_rows = conds.shape[0]
    tn = 1536
    return pl.pallas_call(
        _mod_kernel,
        grid=(DEPTH, 6 * D_MODEL // tn),
        in_specs=[
            pl.BlockSpec((n_rows, D_MODEL), lambda l, n: (0, 0)),
            pl.BlockSpec((None, D_MODEL, tn), lambda l, n: (l, 0, n)),
            pl.BlockSpec((None, 1, tn), lambda l, n: (l, 0, n)),
        ],
        out_specs=pl.BlockSpec((None, n_rows, tn), lambda l, n: (l, 0, n)),
        out_shape=jax.ShapeDtypeStruct((DEPTH, n_rows, 6 * D_MODEL), F32),
        compiler_params=pltpu.CompilerParams(
            dimension_semantics=("arbitrary", "arbitrary"), vmem_limit_bytes=VMEM_LIMIT),
        name="mod_table",
    )(conds, ada_w, ada_b.reshape(DEPTH, 1, 6 * D_MODEL))


def _head_norm(t, g):
    ms = jnp.sum(t * t, axis=-1, keepdims=True) * (1.0 / QK_HEAD)
    return t * lax.rsqrt(ms + EPS) * g


def _value_rows(vt):
    ones = jnp.ones((ONES_ROWS, vt.shape[1]), BF16)
    parts = []
    for h in range(N_HEADS):
        parts += [vt[h * V_HEAD:(h + 1) * V_HEAD, :].astype(BF16), ones]
    return jnp.concatenate(parts, axis=0)


def _cache_keys_kernel(ckv_ref, kr_ref, wuk_ref, wuvt_ref, gk_ref, k_ref, vt_ref):
    ckv = ckv_ref[...].astype(BF16)
    kn = _dot(ckv, wuk_ref[...])
    vt_ref[...] = _value_rows(_dot_nt(wuvt_ref[...], ckv))
    kr = kr_ref[...]
    for h in range(N_HEADS):
        sl = slice(h * LANE, (h + 1) * LANE)
        k_ref[:, sl] = _head_norm(kn[:, sl] + kr, gk_ref[...]).astype(BF16)


def _cache_keys(cache_ckv, cache_kr_pad, wuk, wuvt, gk):
    nb, _, n, _ = cache_ckv.shape
    return pl.pallas_call(
        _cache_keys_kernel,
        grid=(DEPTH, nb),
        in_specs=[
            pl.BlockSpec((None, None, n, KV_RANK), lambda l, b: (b, l, 0, 0)),
            pl.BlockSpec((None, None, n, LANE), lambda l, b: (b, l, 0, 0)),
            pl.BlockSpec((None, KV_RANK, HEAD_PAD), lambda l, b: (l, 0, 0)),
            pl.BlockSpec((None, ATTN_WIDTH, KV_RANK), lambda l, b: (l, 0, 0)),
            pl.BlockSpec((None, 1, LANE), lambda l, b: (l, 0, 0)),
        ],
        out_specs=[
            pl.BlockSpec((None, None, n, HEAD_PAD), lambda l, b: (l, b, 0, 0)),
            pl.BlockSpec((None, None, VT_ROWS, n), lambda l, b: (l, b, 0, 0)),
        ],
        out_shape=[jax.ShapeDtypeStruct((DEPTH, nb, n, HEAD_PAD), BF16),
                   jax.ShapeDtypeStruct((DEPTH, nb, VT_ROWS, n), BF16)],
        compiler_params=pltpu.CompilerParams(
            dimension_semantics=("arbitrary", "arbitrary"), vmem_limit_bytes=VMEM_LIMIT),
        name="cache_keys",
    )(cache_ckv, cache_kr_pad, wuk, wuvt, gk)


def _lanes(col_ref, n):
    return jnp.concatenate([col_ref[...]] * (n // LANE), axis=1)


def _rms_t(t, g):
    return t * lax.rsqrt(jnp.mean(t * t, axis=0, keepdims=True) + EPS) * g


def _head_t(t, g, rope):
    t = t * lax.rsqrt(jnp.mean(t * t, axis=0, keepdims=True) + EPS) * g
    if rope is not None:
        cos, sin = rope
        half = QK_ROPE // 2
        r1 = t[QK_NOPE:QK_NOPE + half, :]
        r2 = t[QK_NOPE + half:, :]
        t = jnp.concatenate([t[:QK_NOPE, :], r1 * cos - r2 * sin, r1 * sin + r2 * cos], axis=0)
    return t


def _inproj_kernel(*refs, latent):
    if latent:
        (_, x_ref, mod_ref, g1_ref, wu_ref, wrt_ref, qag_ref, wuqt_ref, kvg_ref, wukt_ref, wuvt_ref, gq_ref, gk_ref,
         rope_ref, u_ref, q_ref, k_ref, vt_ref) = refs
    else:
        (_, x_ref, mod_ref, g1_ref, wu_ref, wrt_ref, qag_ref, wuqt_ref, kvg_ref, wukt_ref, wuvt_ref, gq_ref, gk_ref,
         u_ref, q_ref, k_ref, vt_ref, ckv_ref, kr_ref) = refs
    n = x_ref.shape[0] // N_SUB
    shift1 = mod_ref[0:1, :]
    scale1 = mod_ref[1:2, :]
    gain1 = g1_ref[...] * (1.0 + scale1)
    zeros = jnp.zeros((LANE - QK_HEAD, n), F32)

    def modulate(rows):
        return (_rms(x_ref[rows, :], gain1) + shift1).astype(BF16)

    def project(rows, h):
        u_ref[rows, :] = _dot(h, wu_ref[...])
        return _dot_nt(wrt_ref[...], h)

    def latent_norms(pt):
        cq = _rms_t(pt[:Q_RANK, :], _lanes(qag_ref, n))
        ckv = _rms_t(pt[Q_RANK:Q_RANK + KV_RANK, :], _lanes(kvg_ref, n))
        return cq.astype(BF16), ckv, pt[Q_RANK + KV_RANK:, :]

    def up_project(rows, cq_b, ckv):
        ckv_b = ckv.astype(BF16)
        qt = _dot(wuqt_ref[...], cq_b)
        knt = _dot(wukt_ref[...], ckv_b)
        vt_ref[:, rows] = _value_rows(_dot(wuvt_ref[...], ckv_b))
        return qt, knt

    def heads(rows, qt, knt, ckv, krt):
        rope = (rope_ref[0:QK_ROPE // 2, rows], rope_ref[QK_ROPE // 2:QK_ROPE, rows]) if latent else None
        gq = _lanes(gq_ref, n)
        gk = _lanes(gk_ref, n)
        k_heads = []
        for hd in range(N_HEADS):
            qh = _head_t(qt[hd * QK_HEAD:(hd + 1) * QK_HEAD, :], gq, rope)
            q_ref[hd * LANE:hd * LANE + QK_HEAD, rows] = qh.astype(BF16)
            q_ref[hd * LANE + QK_HEAD:(hd + 1) * LANE, rows] = zeros.astype(BF16)
            kh = jnp.concatenate([knt[hd * QK_NOPE:(hd + 1) * QK_NOPE, :], krt], axis=0)
            k_heads += [_head_t(kh, gk, rope), zeros]
        k_ref[rows, :] = jnp.concatenate(k_heads, axis=0).T.astype(BF16)
        if not latent:
            ckv_ref[rows, :] = ckv.T
            kr_ref[rows, :] = jnp.concatenate([jnp.zeros((QK_NOPE, n), F32), krt, zeros], axis=0).T

    subs = [slice(i * n, (i + 1) * n) for i in range(N_SUB)]
    hs = [modulate(r) for r in subs]
    pts = [project(r, h) for r, h in zip(subs, hs)]
    normed = [latent_norms(pt) for pt in pts]
    ups = [up_project(r, cq_b, ckv) for r, (cq_b, ckv, _) in zip(subs, normed)]
    for r, (qt, knt), (_, ckv, krt) in zip(subs, ups, normed):
        heads(r, qt, knt, ckv, krt)


def _inproj(layer, x, mod, w, *, latent, seq_len, rope=None):
    t = x.shape[0]
    tb = min(PROJ_BLOCK, seq_len)
    bps = seq_len // tb
    rest = Q_RANK + KV_RANK + QK_ROPE
    in_specs = [
        pl.BlockSpec((tb, D_MODEL), lambda j, l: (j, 0)),
        _cond_spec(latent, bps),
        _layer_spec((1, D_MODEL)),
        _layer_spec((D_MODEL, POOL_WIDTH)),
        _layer_spec((rest, D_MODEL)),
        _layer_spec((Q_RANK, LANE)),
        _layer_spec((N_HEADS * QK_HEAD, Q_RANK)),
        _layer_spec((KV_RANK, LANE)),
        _layer_spec((N_HEADS * QK_NOPE, KV_RANK)),
        _layer_spec((ATTN_WIDTH, KV_RANK)),
        _layer_spec((QK_HEAD, LANE)),
        _layer_spec((QK_HEAD, LANE)),
    ]
    args = [x, mod, w['g1'], w['w_u'], w['w_rest_t'], w['qag_col'], w['w_uq_t'], w['kvg_col'], w['w_uk_t'],
            w['w_uvt'], w['gq_col'], w['gk_col']]
    tok = lambda width: pl.BlockSpec((tb, width), lambda j, l: (j, 0))
    vt_spec = pl.BlockSpec((None, VT_ROWS, tb), lambda j, l: (j // bps, 0, j % bps))
    out_specs = [tok(POOL_WIDTH), pl.BlockSpec((HEAD_PAD, tb), lambda j, l: (0, j)), tok(HEAD_PAD), vt_spec]
    out_shape = [jax.ShapeDtypeStruct((t, POOL_WIDTH), F32), jax.ShapeDtypeStruct((HEAD_PAD, t), BF16),
                 jax.ShapeDtypeStruct((t, HEAD_PAD), BF16), jax.ShapeDtypeStruct((t // seq_len, VT_ROWS, seq_len), BF16)]
    if latent:
        in_specs.append(pl.BlockSpec((QK_ROPE, tb), lambda j, l: (0, j % bps)))
        args.append(rope)
    else:
        out_specs += [tok(KV_RANK), tok(LANE)]
        out_shape += [jax.ShapeDtypeStruct((t, KV_RANK), F32), jax.ShapeDtypeStruct((t, LANE), F32)]
    return _call(functools.partial(_inproj_kernel, latent=latent), layer, (t // tb,), in_specs, out_specs, out_shape,
                 args, "inproj_latent" if latent else "inproj_context")


def _attn_kernel(*refs, n_seg):
    q_ref = refs[1]
    o_ref = refs[-1]
    tb = TOKEN_BLOCK
    tiles = [(s, c) for s in range(n_seg) for c in range(refs[2 + 2 * s].shape[0] // tb)]

    def fold(x):
        return x.reshape(x.shape[0] // 8, 8, x.shape[1])

    def score_tile(hd, s, c):
        sl = slice(hd * LANE, (hd + 1) * LANE)
        st = _dot(refs[2 + 2 * s][c * tb:(c + 1) * tb, sl], q_ref[sl, :])
        return st, jnp.max(fold(st), axis=0)

    def running_max(m8, mt):
        return mt if m8 is None else jnp.maximum(m8, mt)

    st_cur, m8 = [], None
    for s, c in tiles:
        st, mt = score_tile(0, s, c)
        st_cur.append(st)
        m8 = running_max(m8, mt)
    outs = []
    for hd in range(N_HEADS):
        vs = slice(hd * V_ROWS, (hd + 1) * V_ROWS)
        m = jnp.max(m8, axis=0, keepdims=True)
        st_next, m8 = [], None
        acc = None
        pending = []
        probes = []

        def pv_step(acc, item):
            pb, s, c = item
            pv = _dot(refs[3 + 2 * s][vs, c * tb:(c + 1) * tb], pb)
            return pv if acc is None else acc + pv

        for i, (s, c) in enumerate(tiles):
            m_i = m
            if hd + 1 < N_HEADS:
                st, mt = score_tile(hd + 1, s, c)
                st_next.append(st)
                m8 = running_max(m8, mt)
                probes.append(mt[0:1, :])
                if i >= EXP_LAG:
                    probe = probes[i - EXP_LAG]
                    m_i = jnp.where(probe != probe, probe, m)
            if len(pending) >= PV_LAG:
                acc = pv_step(acc, pending.pop(0))
            pending.append((jnp.exp2((st_cur[i] - m_i).astype(BF16)), s, c))
        for item in pending:
            acc = pv_step(acc, item)
        outs.append(acc[:V_HEAD, :] * (1.0 / acc[V_HEAD:V_HEAD + 1, :]))
        st_cur = st_next
    o_ref[...] = jnp.concatenate(outs, axis=0).T.astype(BF16)


def _attention(layer, q, segs, *, seq_len):
    t = q.shape[1]
    tb = TOKEN_BLOCK
    bps = seq_len // tb
    in_specs = [pl.BlockSpec((HEAD_PAD, tb), lambda j, l: (0, j))]
    args = [q]
    for k, vt, per_layer in segs:
        n = k.shape[-2]
        if per_layer:
            in_specs += [pl.BlockSpec((None, None, n, HEAD_PAD), lambda j, l: (l[0], j // bps, 0, 0)),
                         pl.BlockSpec((None, None, VT_ROWS, n), lambda j, l: (l[0], j // bps, 0, 0))]
        else:
            in_specs += [pl.BlockSpec((None, n, HEAD_PAD), lambda j, l: (j // bps, 0, 0)),
                         pl.BlockSpec((None, VT_ROWS, n), lambda j, l: (j // bps, 0, 0))]
        args += [k, vt]
    return _call(functools.partial(_attn_kernel, n_seg=len(segs)), layer, (t // tb,), in_specs,
                 pl.BlockSpec((tb, ATTN_WIDTH), lambda j, l: (j, 0)), jax.ShapeDtypeStruct((t, ATTN_WIDTH), BF16),
                 args, "attention_%dseg" % len(segs))


def _route(logits_t, bias):
    scores = _sigmoid(logits_t)
    sel = scores + bias
    s_rows = [scores[e:e + 1, :] for e in range(N_EXPERTS)]
    r = [sel[e:e + 1, :] for e in range(N_EXPERTS)]
    picked = []
    group_score = []
    for g in range(N_GROUPS):
        members = range(g * EXPERTS_PER_GROUP, (g + 1) * EXPERTS_PER_GROUP)
        total = None
        for e in members:
            rank = None
            for j in members:
                if j == e:
                    continue
                ahead = (r[j] >= r[e]) if j < e else (r[j] > r[e])
                ahead = ahead.astype(F32)
                rank = ahead if rank is None else rank + ahead
            pick = rank < 2.0
            picked.append(pick)
            contrib = jnp.where(pick, r[e], 0.0)
            total = contrib if total is None else total + contrib
        group_score.append(total)
    gates = []
    for g in range(N_GROUPS):
        best = None
        for j in range(N_GROUPS):
            if j == g:
                continue
            wins = (group_score[g] > group_score[j]) if j < g else (group_score[g] >= group_score[j])
            best = wins if best is None else jnp.logical_and(best, wins)
        for e in range(g * EXPERTS_PER_GROUP, (g + 1) * EXPERTS_PER_GROUP):
            gates.append(jnp.where(jnp.logical_and(best, picked[e]), s_rows[e], 0.0))
    denom = functools.reduce(jnp.add, gates)
    inv = 1.0 / denom
    return [g * inv for g in gates]


def _post_kernel(_, x_ref, up_ref, uc_ref, un_ref, attn_ref, mod_ref, poolw_ref, pscale_ref, wout_ref, g2_ref,
                 rwt_ref, rb_ref, x1_ref, h2_ref, gates_ref, uext_ref, *, blocks_per_seq, seq_len):
    tb = x_ref.shape[0]
    halo = POOL_HALO
    jb = pl.program_id(0) % blocks_per_seq
    uext_ref[0:halo, :] = jnp.where(jb == 0, 0.0, up_ref[tb - halo:tb, :])
    uext_ref[halo:halo + tb, :] = uc_ref[...]
    uext_ref[halo + tb:, :] = jnp.where(jb == blocks_per_seq - 1, 0.0, un_ref[0:halo, :])
    n = tb // N_SUB
    gate1 = mod_ref[2:3, :]
    shift2 = mod_ref[3:4, :]
    gain2 = g2_ref[...] * (1.0 + mod_ref[4:5, :])

    def pool_diffs(r0):
        pos = jb * tb + r0 + lax.broadcasted_iota(jnp.int32, (n, 1), 0)
        diffs = []
        for g, w in enumerate(POOL_WINDOWS):
            half = w // 2
            cols = slice(g * POOL_GROUP, (g + 1) * POOL_GROUP)
            total = None
            for d in range(-half, half):
                part = uext_ref[halo + r0 + d:halo + r0 + d + n, cols]
                total = part if total is None else total + part
            count = jnp.minimum(pos + half, seq_len) - jnp.maximum(pos - half, 0)
            diffs.append((total * (1.0 / count.astype(F32)) - uc_ref[r0:r0 + n, cols]).astype(BF16))
        return diffs

    def mix(r0, diffs):
        pooled = [_dot(d, poolw_ref[g]) for g, d in enumerate(diffs)]
        pool = jnp.concatenate(pooled, axis=-1) * pscale_ref[...]
        return _dot(jnp.concatenate([pool.astype(BF16), attn_ref[r0:r0 + n, :]], axis=-1), wout_ref[...])

    def residual_norm(r0, mixed):
        x1 = x_ref[r0:r0 + n, :] + gate1 * mixed
        x1_ref[r0:r0 + n, :] = x1
        h2 = (_rms(x1, gain2) + shift2).astype(BF16)
        h2_ref[r0:r0 + n, :] = h2
        return h2

    def gates(r0, logits_t):
        gate_rows = _route(logits_t, rb_ref[...])
        zeros = jnp.zeros((LANE - EXPERT_CHUNK, n), F32)
        pieces = []
        for c in range(N_EXPERTS // EXPERT_CHUNK):
            pieces += gate_rows[c * EXPERT_CHUNK:(c + 1) * EXPERT_CHUNK] + [zeros]
        gates_ref[r0:r0 + n, :] = jnp.concatenate(pieces, axis=0).T

    starts = [i * n for i in range(N_SUB)]
    diffs = [pool_diffs(r0) for r0 in starts]
    mixed = [mix(r0, d) for r0, d in zip(starts, diffs)]
    h2s = [residual_norm(r0, m) for r0, m in zip(starts, mixed)]
    logits = [_dot_nt(rwt_ref[...], h2) for h2 in h2s]
    for r0, lt in zip(starts, logits):
        gates(r0, lt)


def _post(layer, x, u, attn, mod, w, router_wt, router_b, *, latent, seq_len):
    t = x.shape[0]
    tb = min(PROJ_BLOCK, seq_len)
    bps = seq_len // tb
    n_chunks = N_EXPERTS // EXPERT_CHUNK
    n_blocks = t // tb
    tok = lambda width: pl.BlockSpec((tb, width), lambda j, l: (j, 0))
    in_specs = [
        tok(D_MODEL),
        pl.BlockSpec((tb, POOL_WIDTH), lambda j, l: (jnp.maximum(j - 1, 0), 0)),
        tok(POOL_WIDTH),
        pl.BlockSpec((tb, POOL_WIDTH), lambda j, l: (jnp.minimum(j + 1, n_blocks - 1), 0)),
        tok(ATTN_WIDTH),
        _cond_spec(latent, bps),
        _layer_spec((len(POOL_WINDOWS), POOL_GROUP, POOL_GROUP)),
        _layer_spec((1, POOL_WIDTH)),
        _layer_spec((2 * ATTN_WIDTH, D_MODEL)),
        _layer_spec((1, D_MODEL)),
        pl.BlockSpec((N_EXPERTS, D_MODEL), lambda j, l: (0, 0)),
        pl.BlockSpec((N_EXPERTS, 1), lambda j, l: (0, 0)),
    ]
    out_shape = [
        jax.ShapeDtypeStruct((t, D_MODEL), F32),
        jax.ShapeDtypeStruct((t, D_MODEL), BF16),
        jax.ShapeDtypeStruct((t, n_chunks * LANE), F32),
    ]
    return _call(functools.partial(_post_kernel, blocks_per_seq=bps, seq_len=seq_len), layer, (n_blocks,), in_specs,
                 [tok(D_MODEL), tok(D_MODEL), tok(n_chunks * LANE)], out_shape,
                 [x, u, u, u, attn, mod, w['pool_w'], w['pool_scale'], w['w_out'], w['g2'], router_wt, router_b],
                 "post_latent" if latent else "post_context",
                 scratch_shapes=[pltpu.VMEM((tb + 2 * POOL_HALO, POOL_WIDTH), F32)])


def _moe_kernel(_, h2_ref, gates_ref, x1_ref, mod_ref, wg_ref, wu_ref, wd_ref, o_ref, acc_ref):
    c = pl.program_id(1)

    @pl.when(c == 0)
    def _():
        acc_ref[...] = jnp.zeros_like(acc_ref)

    h = h2_ref[...]

    def gate_up(e):
        return _dot(h, wg_ref[e]), _dot(h, wu_ref[e])

    total = None
    ab = gate_up(0)
    for e in range(EXPERT_CHUNK):
        ab_next = gate_up(e + 1) if e + 1 < EXPERT_CHUNK else None
        a, b = ab
        act = a * _sigmoid(a) * b * gates_ref[:, e:e + 1]
        y = _dot(act.astype(BF16), wd_ref[e])
        total = y if total is None else total + y
        ab = ab_next
    acc_ref[...] += total

    @pl.when(c == pl.num_programs(1) - 1)
    def _():
        o_ref[...] = x1_ref[...] + mod_ref[5:6, :] * acc_ref[...]


def _moe(layer, h2, gates, x1, mod, w, *, latent, seq_len):
    t = h2.shape[0]
    tm = MOE_BLOCK
    bps = seq_len // tm if latent else 1
    if latent:
        cond = pl.BlockSpec((None, None, 6, D_MODEL), lambda i, c, l: (l[0], 1 + i // bps, 0, 0))
    else:
        cond = pl.BlockSpec((None, None, 6, D_MODEL), lambda i, c, l: (l[0], 0, 0, 0))
    in_specs = [
        pl.BlockSpec((tm, D_MODEL), lambda i, c, l: (i, 0)),
        pl.BlockSpec((tm, LANE), lambda i, c, l: (i, c)),
        pl.BlockSpec((tm, D_MODEL), lambda i, c, l: (i, 0)),
        cond,
        pl.BlockSpec((None, EXPERT_CHUNK, D_MODEL, D_EXPERT), lambda i, c, l: (l[0], c, 0, 0)),
        pl.BlockSpec((None, EXPERT_CHUNK, D_MODEL, D_EXPERT), lambda i, c, l: (l[0], c, 0, 0)),
        pl.BlockSpec((None, EXPERT_CHUNK, D_EXPERT, D_MODEL), lambda i, c, l: (l[0], c, 0, 0)),
    ]
    return _call(_moe_kernel, layer, (t // tm, N_EXPERTS // EXPERT_CHUNK), in_specs,
                 pl.BlockSpec((tm, D_MODEL), lambda i, c, l: (i, 0)), jax.ShapeDtypeStruct((t, D_MODEL), F32),
                 [h2, gates, x1, mod, w['w_gate'], w['w_up'], w['w_down']],
                 "experts_latent" if latent else "experts_context",
                 scratch_shapes=[pltpu.VMEM((tm, D_MODEL), F32)])


def _pad_heads(w, width):
    lead = w.shape[:-1]
    w = w.reshape(lead + (N_HEADS, width))
    w = jnp.pad(w, [(0, 0)] * len(lead) + [(0, 0), (0, LANE - width)])
    return w.reshape(lead + (HEAD_PAD,))


def _pad_gain(g):
    return jnp.pad(g, ((0, 0), (0, LANE - QK_HEAD)))[:, None, :]


def _rope_table(rows):
    row = jnp.repeat(jnp.arange(rows), GRID_W).astype(F32)
    col = jnp.tile(jnp.arange(GRID_W), rows).astype(F32)
    n_freq = QK_ROPE // 4
    inv = ROPE_BASE ** (-jnp.arange(n_freq, dtype=F32) / n_freq)
    ang = jnp.concatenate([row[:, None] * inv, col[:, None] * inv], axis=-1)
    return jnp.concatenate([jnp.cos(ang), jnp.sin(ang)], axis=-1).T


def _col(g, rows):
    g = jnp.pad(g, ((0, 0), (0, rows - g.shape[1])))
    return jnp.broadcast_to(g[:, :, None], g.shape + (LANE,))


def kernel(x_prompt, x_sample, cache_ckv, cache_krope, c, c_ctx, ada_w, ada_b, norm1_g, norm2_g, w_in, pool_w,
           pool_scale, q_a_norm_g, w_uq, kv_a_norm_g, w_ukv, q_norm_g, k_norm_g, w_out, router_w, router_b,
           w_gate, w_up, w_down):
    batch, seq, _ = x_prompt.shape
    dec_batch, dec_seq, _ = x_sample.shape

    conds = jnp.concatenate([c_ctx[None, :], c, jnp.zeros((8 - 1 - dec_batch, D_MODEL), F32)], axis=0)
    mod = _mod_table(conds, ada_w, ada_b).reshape(DEPTH, 8, 6, D_MODEL)

    w_ukv_h = w_ukv.reshape(DEPTH, KV_RANK, N_HEADS, QK_NOPE + V_HEAD)
    w_uk = w_ukv_h[..., :QK_NOPE].reshape(DEPTH, KV_RANK, N_HEADS * QK_NOPE)
    q_scale = math.log2(math.e) / math.sqrt(QK_HEAD)
    weights = {
        'g1': norm1_g[:, None, :],
        'g2': norm2_g[:, None, :],
        'w_u': w_in[:, :, :POOL_WIDTH].astype(BF16),
        'w_rest_t': jnp.swapaxes(w_in[:, :, POOL_WIDTH:], 1, 2).astype(BF16),
        'qag_col': _col(q_a_norm_g, Q_RANK),
        'kvg_col': _col(kv_a_norm_g, KV_RANK),
        'w_uq_t': jnp.swapaxes(w_uq, 1, 2).astype(BF16),
        'w_uk_t': jnp.swapaxes(w_uk, 1, 2).astype(BF16),
        'w_uk': _pad_heads(w_uk, QK_NOPE).astype(BF16),
        'w_uvt': jnp.swapaxes(w_ukv_h[..., QK_NOPE:].reshape(DEPTH, KV_RANK, ATTN_WIDTH), 1, 2).astype(BF16),
        'gq_col': _col(q_norm_g * q_scale, QK_HEAD),
        'gk_col': _col(k_norm_g, QK_HEAD),
        'gk': _pad_gain(k_norm_g),
        'pool_w': pool_w.astype(BF16),
        'pool_scale': pool_scale[:, None, :],
        'w_out': w_out.astype(BF16),
        'w_gate': w_gate.astype(BF16),
        'w_up': w_up.astype(BF16),
        'w_down': w_down.astype(BF16),
    }
    router_wt = router_w.T.astype(BF16)
    router_bc = router_b[:, None]
    rope = _rope_table(dec_seq // GRID_W)

    cache_kr_pad = jnp.pad(cache_krope, ((0, 0), (0, 0), (0, 0), (QK_NOPE, LANE - QK_HEAD)))
    k_cache, vt_cache = _cache_keys(cache_ckv, cache_kr_pad, weights['w_uk'], weights['w_uvt'], weights['gk'])

    def layer_step(carry, layer):
        xc, xl = carry
        u, q, k, vt, ckv, kr = _inproj(layer, xc, mod, weights, latent=False, seq_len=seq)
        ctx_seg = (k.reshape(batch, seq, HEAD_PAD), vt, False)
        attn = _attention(layer, q, [ctx_seg], seq_len=seq)
        x1, h2, gates = _post(layer, xc, u, attn, mod, weights, router_wt, router_bc, latent=False, seq_len=seq)
        xc = _moe(layer, h2, gates, x1, mod, weights, latent=False, seq_len=seq)
        u, q, k, vt = _inproj(layer, xl, mod, weights, latent=True, seq_len=dec_seq, rope=rope)
        lat_seg = (k.reshape(dec_batch, dec_seq, HEAD_PAD), vt, False)
        attn = _attention(layer, q, [(k_cache, vt_cache, True), lat_seg], seq_len=dec_seq)
        x1, h2, gates = _post(layer, xl, u, attn, mod, weights, router_wt, router_bc, latent=True, seq_len=dec_seq)
        xl = _moe(layer, h2, gates, x1, mod, weights, latent=True, seq_len=dec_seq)
        return (xc, xl), (ckv, kr)

    layers = jnp.arange(DEPTH, dtype=jnp.int32).reshape(DEPTH, 1)
    (xc, xl), (ckv_all, kr_all) = lax.scan(
        layer_step, (x_prompt.reshape(batch * seq, D_MODEL), x_sample.reshape(dec_batch * dec_seq, D_MODEL)), layers,
        unroll=True)

    new_ckv = jnp.swapaxes(ckv_all.reshape(DEPTH, batch, seq, KV_RANK), 0, 1)
    new_krope = jnp.swapaxes(kr_all[:, :, QK_NOPE:QK_HEAD].reshape(DEPTH, batch, seq, QK_ROPE), 0, 1)
    return (xc.reshape(batch, seq, D_MODEL), xl.reshape(dec_batch, dec_seq, D_MODEL), new_ckv, new_krope)
```

```python
import functools
import math

import jax
import jax.numpy as jnp
from jax import lax
from jax.experimental import pallas as pl
from jax.experimental.pallas import tpu as pltpu

D_MODEL = 1024
DEPTH = 4
GRID_W = 64
POOL_WINDOWS = (2, 4, 8, 16)
POOL_GROUP = 128
POOL_WIDTH = POOL_GROUP * len(POOL_WINDOWS)
N_HEADS = 8
QK_NOPE = 64
QK_ROPE = 32
QK_HEAD = QK_NOPE + QK_ROPE
V_HEAD = 64
Q_RANK = 384
KV_RANK = 256
ATTN_WIDTH = N_HEADS * V_HEAD
N_EXPERTS = 16
N_GROUPS = 4
EXPERTS_PER_GROUP = N_EXPERTS // N_GROUPS
D_EXPERT = 256
ROPE_BASE = 10000.0
EPS = 1e-6

LANE = 128
HEAD_PAD = N_HEADS * LANE
ONES_ROWS = 16
V_ROWS = V_HEAD + ONES_ROWS
VT_ROWS = N_HEADS * V_ROWS
POOL_HALO = max(POOL_WINDOWS)
TOKEN_BLOCK = 256
EXP_LAG = 3
PV_LAG = 2
PROJ_BLOCK = 1024
N_SUB = 2
MOE_BLOCK = 1024
EXPERT_CHUNK = 4
VMEM_LIMIT = 56 * 1024 * 1024

BF16 = jnp.bfloat16
F32 = jnp.float32


def _dot(a, b):
    return jnp.dot(a, b, preferred_element_type=F32)


def _dot_nt(a, b):
    return lax.dot_general(a, b, (((1,), (1,)), ((), ())), preferred_element_type=F32)


def _rms(x, g):
    return x * lax.rsqrt(jnp.mean(x * x, axis=-1, keepdims=True) + EPS) * g


def _sigmoid(x):
    return 1.0 / (1.0 + jnp.exp(-x))


def _call(body, layer, grid, in_specs, out_specs, out_shape, args, name, scratch_shapes=(), flags=None):
    return pl.pallas_call(
        body,
        grid_spec=pltpu.PrefetchScalarGridSpec(
            num_scalar_prefetch=1, grid=grid, in_specs=in_specs, out_specs=out_specs, scratch_shapes=scratch_shapes),
        out_shape=out_shape,
        compiler_params=pltpu.CompilerParams(
            dimension_semantics=("arbitrary",) * len(grid), vmem_limit_bytes=VMEM_LIMIT, flags=flags),
        name=name,
    )(layer, *args)


def _layer_spec(shape):
    return pl.BlockSpec((None,) + shape, lambda *a: (a[-1][0],) + (0,) * len(shape))


def _cond_spec(latent, blocks_per_seq):
    if latent:
        return pl.BlockSpec((None, None, 6, D_MODEL), lambda j, *a: (a[-1][0], 1 + j // blocks_per_seq, 0, 0))
    return pl.BlockSpec((None, None, 6, D_MODEL), lambda j, *a: (a[-1][0], 0, 0, 0))


def _mod_kernel(cond_ref, w_ref, b_ref, o_ref):
    cond = cond_ref[...]
    s = (cond * _sigmoid(cond)).astype(BF16)
    o_ref[...] = _dot(s, w_ref[...].astype(BF16)) + b_ref[...]


def _mod_table(conds, ada_w, ada_b):
    n_rows = conds.shape[0]
    tn = 1536
    return pl.pallas_call(
        _mod_kernel,
        grid=(DEPTH, 6 * D_MODEL // tn),
        in_specs=[
            pl.BlockSpec((n_rows, D_MODEL), lambda l, n: (0, 0)),
            pl.BlockSpec((None, D_MODEL, tn), lambda l, n: (l, 0, n)),
            pl.BlockSpec((None, 1, tn), lambda l, n: (l, 0, n)),
        ],
        out_specs=pl.BlockSpec((None, n_rows, tn), lambda l, n: (l, 0, n)),
        out_shape=jax.ShapeDtypeStruct((DEPTH, n_rows, 6 * D_MODEL), F32),
        compiler_params=pltpu.CompilerParams(
            dimension_semantics=("arbitrary", "arbitrary"), vmem_limit_bytes=VMEM_LIMIT),
        name="mod_table",
    )(conds, ada_w, ada_b.reshape(DEPTH, 1, 6 * D_MODEL))


def _head_norm(t, g):
    ms = jnp.sum(t * t, axis=-1, keepdims=True) * (1.0 / QK_HEAD)
    return t * lax.rsqrt(ms + EPS) * g


def _value_rows(vt):
    ones = jnp.ones((ONES_ROWS, vt.shape[1]), BF16)
    parts = []
    for h in range(N_HEADS):
        parts += [vt[h * V_HEAD:(h + 1) * V_HEAD, :].astype(BF16), ones]
    return jnp.concatenate(parts, axis=0)


def _cache_keys_kernel(ckv_ref, kr_ref, wuk_ref, wuvt_ref, gk_ref, k_ref, vt_ref):
    ckv = ckv_ref[...].astype(BF16)
    kn = _dot(ckv, wuk_ref[...])
    vt_ref[...] = _value_rows(_dot_nt(wuvt_ref[...], ckv))
    kr = kr_ref[...]
    for h in range(N_HEADS):
        sl = slice(h * LANE, (h + 1) * LANE)
        k_ref[:, sl] = _head_norm(kn[:, sl] + kr, gk_ref[...]).astype(BF16)


def _cache_keys(cache_ckv, cache_kr_pad, wuk, wuvt, gk):
    nb, _, n, _ = cache_ckv.shape
    return pl.pallas_call(
        _cache_keys_kernel,
        grid=(DEPTH, nb),
        in_specs=[
            pl.BlockSpec((None, None, n, KV_RANK), lambda l, b: (b, l, 0, 0)),
            pl.BlockSpec((None, None, n, LANE), lambda l, b: (b, l, 0, 0)),
            pl.BlockSpec((None, KV_RANK, HEAD_PAD), lambda l, b: (l, 0, 0)),
            pl.BlockSpec((None, ATTN_WIDTH, KV_RANK), lambda l, b: (l, 0, 0)),
            pl.BlockSpec((None, 1, LANE), lambda l, b: (l, 0, 0)),
        ],
        out_specs=[
            pl.BlockSpec((None, None, n, HEAD_PAD), lambda l, b: (l, b, 0, 0)),
            pl.BlockSpec((None, None, VT_ROWS, n), lambda l, b: (l, b, 0, 0)),
        ],
        out_shape=[jax.ShapeDtypeStruct((DEPTH, nb, n, HEAD_PAD), BF16),
                   jax.ShapeDtypeStruct((DEPTH, nb, VT_ROWS, n), BF16)],
        compiler_params=pltpu.CompilerParams(
            dimension_semantics=("arbitrary", "arbitrary"), vmem_limit_bytes=VMEM_LIMIT),
        name="cache_keys",
    )(cache_ckv, cache_kr_pad, wuk, wuvt, gk)


def _lanes(col_ref, n):
    return jnp.concatenate([col_ref[...]] * (n // LANE), axis=1)


def _rms_t(t, g):
    return t * lax.rsqrt(jnp.mean(t * t, axis=0, keepdims=True) + EPS) * g


def _head_t(t, g, rope):
    t = t * lax.rsqrt(jnp.mean(t * t, axis=0, keepdims=True) + EPS) * g
    if rope is not None:
        cos, sin = rope
        half = QK_ROPE // 2
        r1 = t[QK_NOPE:QK_NOPE + half, :]
        r2 = t[QK_NOPE + half:, :]
        t = jnp.concatenate([t[:QK_NOPE, :], r1 * cos - r2 * sin, r1 * sin + r2 * cos], axis=0)
    return t


def _inproj_kernel(*refs, latent):
    if latent:
        (_, x_ref, mod_ref, g1_ref, wu_ref, wrt_ref, qag_ref, wuqt_ref, kvg_ref, wukt_ref, wuvt_ref, gq_ref, gk_ref,
         rope_ref, u_ref, q_ref, k_ref, vt_ref) = refs
    else:
        (_, x_ref, mod_ref, g1_ref, wu_ref, wrt_ref, qag_ref, wuqt_ref, kvg_ref, wukt_ref, wuvt_ref, gq_ref, gk_ref,
         u_ref, q_ref, k_ref, vt_ref, ckv_ref, kr_ref) = refs
    n = x_ref.shape[0] // N_SUB
    shift1 = mod_ref[0:1, :]
    scale1 = mod_ref[1:2, :]
    gain1 = g1_ref[...] * (1.0 + scale1)
    zeros = jnp.zeros((LANE - QK_HEAD, n), F32)

    def modulate(rows):
        return (_rms(x_ref[rows, :], gain1) + shift1).astype(BF16)

    def project(rows, h):
        u_ref[rows, :] = _dot(h, wu_ref[...])
        return _dot_nt(wrt_ref[...], h)

    def latent_norms(pt):
        cq = _rms_t(pt[:Q_RANK, :], _lanes(qag_ref, n))
        ckv = _rms_t(pt[Q_RANK:Q_RANK + KV_RANK, :], _lanes(kvg_ref, n))
        return cq.astype(BF16), ckv, pt[Q_RANK + KV_RANK:, :]

    def up_project(rows, cq_b, ckv):
        ckv_b = ckv.astype(BF16)
        qt = _dot(wuqt_ref[...], cq_b)
        knt = _dot(wukt_ref[...], ckv_b)
        vt_ref[:, rows] = _value_rows(_dot(wuvt_ref[...], ckv_b))
        return qt, knt

    def heads(rows, qt, knt, ckv, krt):
        rope = (rope_ref[0:QK_ROPE // 2, rows], rope_ref[QK_ROPE // 2:QK_ROPE, rows]) if latent else None
        gq = _lanes(gq_ref, n)
        gk = _lanes(gk_ref, n)
        k_heads = []
        for hd in range(N_HEADS):
            qh = _head_t(qt[hd * QK_HEAD:(hd + 1) * QK_HEAD, :], gq, rope)
            q_ref[hd * LANE:hd * LANE + QK_HEAD, rows] = qh.astype(BF16)
            q_ref[hd * LANE + QK_HEAD:(hd + 1) * LANE, rows] = zeros.astype(BF16)
            kh = jnp.concatenate([knt[hd * QK_NOPE:(hd + 1) * QK_NOPE, :], krt], axis=0)
            k_heads += [_head_t(kh, gk, rope), zeros]
        k_ref[rows, :] = jnp.concatenate(k_heads, axis=0).T.astype(BF16)
        if not latent:
            ckv_ref[rows, :] = ckv.T
            kr_ref[rows, :] = jnp.concatenate([jnp.zeros((QK_NOPE, n), F32), krt, zeros], axis=0).T

    subs = [slice(i * n, (i + 1) * n) for i in range(N_SUB)]
    hs = [modulate(r) for r in subs]
    pts = [project(r, h) for r, h in zip(subs, hs)]
    normed = [latent_norms(pt) for pt in pts]
    ups = [up_project(r, cq_b, ckv) for r, (cq_b, ckv, _) in zip(subs, normed)]
    for r, (qt, knt), (_, ckv, krt) in zip(subs, ups, normed):
        heads(r, qt, knt, ckv, krt)


def _inproj(layer, x, mod, w, *, latent, seq_len, rope=None):
    t = x.shape[0]
    tb = min(PROJ_BLOCK, seq_len)
    bps = seq_len // tb
    rest = Q_RANK + KV_RANK + QK_ROPE
    in_specs = [
        pl.BlockSpec((tb, D_MODEL), lambda j, l: (j, 0)),
        _cond_spec(latent, bps),
        _layer_spec((1, D_MODEL)),
        _layer_spec((D_MODEL, POOL_WIDTH)),
        _layer_spec((rest, D_MODEL)),
        _layer_spec((Q_RANK, LANE)),
        _layer_spec((N_HEADS * QK_HEAD, Q_RANK)),
        _layer_spec((KV_RANK, LANE)),
        _layer_spec((N_HEADS * QK_NOPE, KV_RANK)),
        _layer_spec((ATTN_WIDTH, KV_RANK)),
        _layer_spec((QK_HEAD, LANE)),
        _layer_spec((QK_HEAD, LANE)),
    ]
    args = [x, mod, w['g1'], w['w_u'], w['w_rest_t'], w['qag_col'], w['w_uq_t'], w['kvg_col'], w['w_uk_t'],
            w['w_uvt'], w['gq_col'], w['gk_col']]
    tok = lambda width: pl.BlockSpec((tb, width), lambda j, l: (j, 0))
    vt_spec = pl.BlockSpec((None, VT_ROWS, tb), lambda j, l: (j // bps, 0, j % bps))
    out_specs = [tok(POOL_WIDTH), pl.BlockSpec((HEAD_PAD, tb), lambda j, l: (0, j)), tok(HEAD_PAD), vt_spec]
    out_shape = [jax.ShapeDtypeStruct((t, POOL_WIDTH), F32), jax.ShapeDtypeStruct((HEAD_PAD, t), BF16),
                 jax.ShapeDtypeStruct((t, HEAD_PAD), BF16), jax.ShapeDtypeStruct((t // seq_len, VT_ROWS, seq_len), BF16)]
    if latent:
        in_specs.append(pl.BlockSpec((QK_ROPE, tb), lambda j, l: (0, j % bps)))
        args.append(rope)
    else:
        out_specs += [tok(KV_RANK), tok(LANE)]
        out_shape += [jax.ShapeDtypeStruct((t, KV_RANK), F32), jax.ShapeDtypeStruct((t, LANE), F32)]
    return _call(functools.partial(_inproj_kernel, latent=latent), layer, (t // tb,), in_specs, out_specs, out_shape,
                 args, "inproj_latent" if latent else "inproj_context")


def _attn_kernel(*refs, n_seg):
    q_ref = refs[1]
    o_ref = refs[-1]
    tb = TOKEN_BLOCK
    tiles = [(s, c) for s in range(n_seg) for c in range(refs[2 + 2 * s].shape[0] // tb)]

    def fold(x):
        return x.reshape(x.shape[0] // 8, 8, x.shape[1])

    def score_tile(hd, s, c):
        sl = slice(hd * LANE, (hd + 1) * LANE)
        st = _dot(refs[2 + 2 * s][c * tb:(c + 1) * tb, sl], q_ref[sl, :])
        return st, jnp.max(fold(st), axis=0)

    def running_max(m8, mt):
        return mt if m8 is None else jnp.maximum(m8, mt)

    st_cur, m8 = [], None
    for s, c in tiles:
        st, mt = score_tile(0, s, c)
        st_cur.append(st)
        m8 = running_max(m8, mt)
    outs = []
    for hd in range(N_HEADS):
        vs = slice(hd * V_ROWS, (hd + 1) * V_ROWS)
        m = jnp.max(m8, axis=0, keepdims=True)
        st_next, m8 = [], None
        acc = None
        pending = []
        probes = []

        def pv_step(acc, item):
            pb, s, c = item
            pv = _dot(refs[3 + 2 * s][vs, c * tb:(c + 1) * tb], pb)
            return pv if acc is None else acc + pv

        for i, (s, c) in enumerate(tiles):
            m_i = m
            if hd + 1 < N_HEADS:
                st, mt = score_tile(hd + 1, s, c)
                st_next.append(st)
                m8 = running_max(m8, mt)
                probes.append(mt[0:1, :])
                if i >= EXP_LAG:
                    probe = probes[i - EXP_LAG]
                    m_i = jnp.where(probe != probe, probe, m)
            if len(pending) >= PV_LAG:
                acc = pv_step(acc, pending.pop(0))
            pending.append((jnp.exp2((st_cur[i] - m_i).astype(BF16)), s, c))
        for item in pending:
            acc = pv_step(acc, item)
        outs.append(acc[:V_HEAD, :] * (1.0 / acc[V_HEAD:V_HEAD + 1, :]))
        st_cur = st_next
    o_ref[...] = jnp.concatenate(outs, axis=0).T.astype(BF16)


def _attention(layer, q, segs, *, seq_len):
    t = q.shape[1]
    tb = TOKEN_BLOCK
    bps = seq_len // tb
    in_specs = [pl.BlockSpec((HEAD_PAD, tb), lambda j, l: (0, j))]
    args = [q]
    for k, vt, per_layer in segs:
        n = k.shape[-2]
        if per_layer:
            in_specs += [pl.BlockSpec((None, None, n, HEAD_PAD), lambda j, l: (l[0], j // bps, 0, 0)),
                         pl.BlockSpec((None, None, VT_ROWS, n), lambda j, l: (l[0], j // bps, 0, 0))]
        else:
            in_specs += [pl.BlockSpec((None, n, HEAD_PAD), lambda j, l: (j // bps, 0, 0)),
                         pl.BlockSpec((None, VT_ROWS, n), lambda j, l: (j // bps, 0, 0))]
        args += [k, vt]
    return _call(functools.partial(_attn_kernel, n_seg=len(segs)), layer, (t // tb,), in_specs,
                 pl.BlockSpec((tb, ATTN_WIDTH), lambda j, l: (j, 0)), jax.ShapeDtypeStruct((t, ATTN_WIDTH), BF16),
                 args, "attention_%dseg" % len(segs))


def _route(logits_t, bias):
    scores = _sigmoid(logits_t)
    sel = scores + bias
    s_rows = [scores[e:e + 1, :] for e in range(N_EXPERTS)]
    r = [sel[e:e + 1, :] for e in range(N_EXPERTS)]
    picked = []
    group_score = []
    for g in range(N_GROUPS):
        members = range(g * EXPERTS_PER_GROUP, (g + 1) * EXPERTS_PER_GROUP)
        total = None
        for e in members:
            rank = None
            for j in members:
                if j == e:
                    continue
                ahead = (r[j] >= r[e]) if j < e else (r[j] > r[e])
                ahead = ahead.astype(F32)
                rank = ahead if rank is None else rank + ahead
            pick = rank < 2.0
            picked.append(pick)
            contrib = jnp.where(pick, r[e], 0.0)
            total = contrib if total is None else total + contrib
        group_score.append(total)
    gates = []
    for g in range(N_GROUPS):
        best = None
        for j in range(N_GROUPS):
            if j == g:
                continue
            wins = (group_score[g] > group_score[j]) if j < g else (group_score[g] >= group_score[j])
            best = wins if best is None else jnp.logical_and(best, wins)
        for e in range(g * EXPERTS_PER_GROUP, (g + 1) * EXPERTS_PER_GROUP):
            gates.append(jnp.where(jnp.logical_and(best, picked[e]), s_rows[e], 0.0))
    denom = functools.reduce(jnp.add, gates)
    inv = 1.0 / denom
    return [g * inv for g in gates]


def _post_kernel(_, x_ref, up_ref, uc_ref, un_ref, attn_ref, mod_ref, poolw_ref, pscale_ref, wout_ref, g2_ref,
                 rwt_ref, rb_ref, x1_ref, h2_ref, gates_ref, uext_ref, *, blocks_per_seq, seq_len):
    tb = x_ref.shape[0]
    halo = POOL_HALO
    jb = pl.program_id(0) % blocks_per_seq
    uext_ref[0:halo, :] = jnp.where(jb == 0, 0.0, up_ref[tb - halo:tb, :])
    uext_ref[halo:halo + tb, :] = uc_ref[...]
    uext_ref[halo + tb:, :] = jnp.where(jb == blocks_per_seq - 1, 0.0, un_ref[0:halo, :])
    n = tb // N_SUB
    gate1 = mod_ref[2:3, :]
    shift2 = mod_ref[3:4, :]
    gain2 = g2_ref[...] * (1.0 + mod_ref[4:5, :])

    def pool_diffs(r0):
        pos = jb * tb + r0 + lax.broadcasted_iota(jnp.int32, (n, 1), 0)
        diffs = []
        for g, w in enumerate(POOL_WINDOWS):
            half = w // 2
            cols = slice(g * POOL_GROUP, (g + 1) * POOL_GROUP)
            total = None
            for d in range(-half, half):
                part = uext_ref[halo + r0 + d:halo + r0 + d + n, cols]
                total = part if total is None else total + part
            count = jnp.minimum(pos + half, seq_len) - jnp.maximum(pos - half, 0)
            diffs.append((total * (1.0 / count.astype(F32)) - uc_ref[r0:r0 + n, cols]).astype(BF16))
        return diffs

    def mix(r0, diffs):
        pooled = [_dot(d, poolw_ref[g]) for g, d in enumerate(diffs)]
        pool = jnp.concatenate(pooled, axis=-1) * pscale_ref[...]
        return _dot(jnp.concatenate([pool.astype(BF16), attn_ref[r0:r0 + n, :]], axis=-1), wout_ref[...])

    def residual_norm(r0, mixed):
        x1 = x_ref[r0:r0 + n, :] + gate1 * mixed
        x1_ref[r0:r0 + n, :] = x1
        h2 = (_rms(x1, gain2) + shift2).astype(BF16)
        h2_ref[r0:r0 + n, :] = h2
        return h2

    def gates(r0, logits_t):
        gate_rows = _route(logits_t, rb_ref[...])
        zeros = jnp.zeros((LANE - EXPERT_CHUNK, n), F32)
        pieces = []
        for c in range(N_EXPERTS // EXPERT_CHUNK):
            pieces += gate_rows[c * EXPERT_CHUNK:(c + 1) * EXPERT_CHUNK] + [zeros]
        gates_ref[r0:r0 + n, :] = jnp.concatenate(pieces, axis=0).T

    starts = [i * n for i in range(N_SUB)]
    diffs = [pool_diffs(r0) for r0 in starts]
    mixed = [mix(r0, d) for r0, d in zip(starts, diffs)]
    h2s = [residual_norm(r0, m) for r0, m in zip(starts, mixed)]
    logits = [_dot_nt(rwt_ref[...], h2) for h2 in h2s]
    for r0, lt in zip(starts, logits):
        gates(r0, lt)


def _post(layer, x, u, attn, mod, w, router_wt, router_b, *, latent, seq_len):
    t = x.shape[0]
    tb = min(PROJ_BLOCK, seq_len)
    bps = seq_len // tb
    n_chunks = N_EXPERTS // EXPERT_CHUNK
    n_blocks = t // tb
    tok = lambda width: pl.BlockSpec((tb, width), lambda j, l: (j, 0))
    in_specs = [
        tok(D_MODEL),
        pl.BlockSpec((tb, POOL_WIDTH), lambda j, l: (jnp.maximum(j - 1, 0), 0)),
        tok(POOL_WIDTH),
        pl.BlockSpec((tb, POOL_WIDTH), lambda j, l: (jnp.minimum(j + 1, n_blocks - 1), 0)),
        tok(ATTN_WIDTH),
        _cond_spec(latent, bps),
        _layer_spec((len(POOL_WINDOWS), POOL_GROUP, POOL_GROUP)),
        _layer_spec((1, POOL_WIDTH)),
        _layer_spec((2 * ATTN_WIDTH, D_MODEL)),
        _layer_spec((1, D_MODEL)),
        pl.BlockSpec((N_EXPERTS, D_MODEL), lambda j, l: (0, 0)),
        pl.BlockSpec((N_EXPERTS, 1), lambda j, l: (0, 0)),
    ]
    out_shape = [
        jax.ShapeDtypeStruct((t, D_MODEL), F32),
        jax.ShapeDtypeStruct((t, D_MODEL), BF16),
        jax.ShapeDtypeStruct((t, n_chunks * LANE), F32),
    ]
    return _call(functools.partial(_post_kernel, blocks_per_seq=bps, seq_len=seq_len), layer, (n_blocks,), in_specs,
                 [tok(D_MODEL), tok(D_MODEL), tok(n_chunks * LANE)], out_shape,
                 [x, u, u, u, attn, mod, w['pool_w'], w['pool_scale'], w['w_out'], w['g2'], router_wt, router_b],
                 "post_latent" if latent else "post_context",
                 scratch_shapes=[pltpu.VMEM((tb + 2 * POOL_HALO, POOL_WIDTH), F32)])


def _moe_kernel(_, h2_ref, gates_ref, x1_ref, mod_ref, wg_ref, wu_ref, wd_ref, o_ref, acc_ref):
    c = pl.program_id(1)

    @pl.when(c == 0)
    def _():
        acc_ref[...] = jnp.zeros_like(acc_ref)

    h = h2_ref[...]

    def gate_up(e):
        return _dot(h, wg_ref[e]), _dot(h, wu_ref[e])

    total = None
    ab = gate_up(0)
    for e in range(EXPERT_CHUNK):
        ab_next = gate_up(e + 1) if e + 1 < EXPERT_CHUNK else None
        a, b = ab
        act = a * _sigmoid(a) * b * gates_ref[:, e:e + 1]
        y = _dot(act.astype(BF16), wd_ref[e])
        total = y if total is None else total + y
        ab = ab_next
    acc_ref[...] += total

    @pl.when(c == pl.num_programs(1) - 1)
    def _():
        o_ref[...] = x1_ref[...] + mod_ref[5:6, :] * acc_ref[...]


def _moe(layer, h2, gates, x1, mod, w, *, latent, seq_len):
    t = h2.shape[0]
    tm = MOE_BLOCK
    bps = seq_len // tm if latent else 1
    if latent:
        cond = pl.BlockSpec((None, None, 6, D_MODEL), lambda i, c, l: (l[0], 1 + i // bps, 0, 0))
    else:
        cond = pl.BlockSpec((None, None, 6, D_MODEL), lambda i, c, l: (l[0], 0, 0, 0))
    in_specs = [
        pl.BlockSpec((tm, D_MODEL), lambda i, c, l: (i, 0)),
        pl.BlockSpec((tm, LANE), lambda i, c, l: (i, c)),
        pl.BlockSpec((tm, D_MODEL), lambda i, c, l: (i, 0)),
        cond,
        pl.BlockSpec((None, EXPERT_CHUNK, D_MODEL, D_EXPERT), lambda i, c, l: (l[0], c, 0, 0)),
        pl.BlockSpec((None, EXPERT_CHUNK, D_MODEL, D_EXPERT), lambda i, c, l: (l[0], c, 0, 0)),
        pl.BlockSpec((None, EXPERT_CHUNK, D_EXPERT, D_MODEL), lambda i, c, l: (l[0], c, 0, 0)),
    ]
    return _call(_moe_kernel, layer, (t // tm, N_EXPERTS // EXPERT_CHUNK), in_specs,
                 pl.BlockSpec((tm, D_MODEL), lambda i, c, l: (i, 0)), jax.ShapeDtypeStruct((t, D_MODEL), F32),
                 [h2, gates, x1, mod, w['w_gate'], w['w_up'], w['w_down']],
                 "experts_latent" if latent else "experts_context",
                 scratch_shapes=[pltpu.VMEM((tm, D_MODEL), F32)])


def _pad_heads(w, width):
    lead = w.shape[:-1]
    w = w.reshape(lead + (N_HEADS, width))
    w = jnp.pad(w, [(0, 0)] * len(lead) + [(0, 0), (0, LANE - width)])
    return w.reshape(lead + (HEAD_PAD,))


def _pad_gain(g):
    return jnp.pad(g, ((0, 0), (0, LANE - QK_HEAD)))[:, None, :]


def _rope_table(rows):
    row = jnp.repeat(jnp.arange(rows), GRID_W).astype(F32)
    col = jnp.tile(jnp.arange(GRID_W), rows).astype(F32)
    n_freq = QK_ROPE // 4
    inv = ROPE_BASE ** (-jnp.arange(n_freq, dtype=F32) / n_freq)
    ang = jnp.concatenate([row[:, None] * inv, col[:, None] * inv], axis=-1)
    return jnp.concatenate([jnp.cos(ang), jnp.sin(ang)], axis=-1).T


def _col(g, rows):
    g = jnp.pad(g, ((0, 0), (0, rows - g.shape[1])))
    return jnp.broadcast_to(g[:, :, None], g.shape + (LANE,))


def kernel(x_prompt, x_sample, cache_ckv, cache_krope, c, c_ctx, ada_w, ada_b, norm1_g, norm2_g, w_in, pool_w,
           pool_scale, q_a_norm_g, w_uq, kv_a_norm_g, w_ukv, q_norm_g, k_norm_g, w_out, router_w, router_b,
           w_gate, w_up, w_down):
    batch, seq, _ = x_prompt.shape
    dec_batch, dec_seq, _ = x_sample.shape

    conds = jnp.concatenate([c_ctx[None, :], c, jnp.zeros((8 - 1 - dec_batch, D_MODEL), F32)], axis=0)
    mod = _mod_table(conds, ada_w, ada_b).reshape(DEPTH, 8, 6, D_MODEL)

    w_ukv_h = w_ukv.reshape(DEPTH, KV_RANK, N_HEADS, QK_NOPE + V_HEAD)
    w_uk = w_ukv_h[..., :QK_NOPE].reshape(DEPTH, KV_RANK, N_HEADS * QK_NOPE)
    q_scale = math.log2(math.e) / math.sqrt(QK_HEAD)
    weights = {
        'g1': norm1_g[:, None, :],
        'g2': norm2_g[:, None, :],
        'w_u': w_in[:, :, :POOL_WIDTH].astype(BF16),
        'w_rest_t': jnp.swapaxes(w_in[:, :, POOL_WIDTH:], 1, 2).astype(BF16),
        'qag_col': _col(q_a_norm_g, Q_RANK),
        'kvg_col': _col(kv_a_norm_g, KV_RANK),
        'w_uq_t': jnp.swapaxes(w_uq, 1, 2).astype(BF16),
        'w_uk_t': jnp.swapaxes(w_uk, 1, 2).astype(BF16),
        'w_uk': _pad_heads(w_uk, QK_NOPE).astype(BF16),
        'w_uvt': jnp.swapaxes(w_ukv_h[..., QK_NOPE:].reshape(DEPTH, KV_RANK, ATTN_WIDTH), 1, 2).astype(BF16),
        'gq_col': _col(q_norm_g * q_scale, QK_HEAD),
        'gk_col': _col(k_norm_g, QK_HEAD),
        'gk': _pad_gain(k_norm_g),
        'pool_w': pool_w.astype(BF16),
        'pool_scale': pool_scale[:, None, :],
        'w_out': w_out.astype(BF16),
        'w_gate': w_gate.astype(BF16),
        'w_up': w_up.astype(BF16),
        'w_down': w_down.astype(BF16),
    }
    router_wt = router_w.T.astype(BF16)
    router_bc = router_b[:, None]
    rope = _rope_table(dec_seq // GRID_W)

    cache_kr_pad = jnp.pad(cache_krope, ((0, 0), (0, 0), (0, 0), (QK_NOPE, LANE - QK_HEAD)))
    k_cache, vt_cache = _cache_keys(cache_ckv, cache_kr_pad, weights['w_uk'], weights['w_uvt'], weights['gk'])

    def layer_step(carry, layer):
        xc, xl = carry
        u, q, k, vt, ckv, kr = _inproj(layer, xc, mod, weights, latent=False, seq_len=seq)
        ctx_seg = (k.reshape(batch, seq, HEAD_PAD), vt, False)
        attn = _attention(layer, q, [ctx_seg], seq_len=seq)
        x1, h2, gates = _post(layer, xc, u, attn, mod, weights, router_wt, router_bc, latent=False, seq_len=seq)
        xc = _moe(layer, h2, gates, x1, mod, weights, latent=False, seq_len=seq)
        u, q, k, vt = _inproj(layer, xl, mod, weights, latent=True, seq_len=dec_seq, rope=rope)
        lat_seg = (k.reshape(dec_batch, dec_seq, HEAD_PAD), vt, False)
        attn = _attention(layer, q, [(k_cache, vt_cache, True), lat_seg], seq_len=dec_seq)
        x1, h2, gates = _post(layer, xl, u, attn, mod, weights, router_wt, router_bc, latent=True, seq_len=dec_seq)
        xl = _moe(layer, h2, gates, x1, mod, weights, latent=True, seq_len=dec_seq)
        return (xc, xl), (ckv, kr)

    layers = jnp.arange(DEPTH, dtype=jnp.int32).reshape(DEPTH, 1)
    (xc, xl), (ckv_all, kr_all) = lax.scan(
        layer_step, (x_prompt.reshape(batch * seq, D_MODEL), x_sample.reshape(dec_batch * dec_seq, D_MODEL)), layers,
        unroll=True)

    new_ckv = jnp.swapaxes(ckv_all.reshape(DEPTH, batch, seq, KV_RANK), 0, 1)
    new_krope = jnp.swapaxes(kr_all[:, :, QK_NOPE:QK_HEAD].reshape(DEPTH, batch, seq, QK_ROPE), 0, 1)
    return (xc.reshape(batch, seq, D_MODEL), xl.reshape(dec_batch, dec_seq, D_MODEL), new_ckv, new_krope)
```

```python
import functools
import math

import jax
import jax.numpy as jnp
from jax import lax
from jax.experimental import pallas as pl
from jax.experimental.pallas import tpu as pltpu

D_MODEL = 1024
DEPTH = 4
GRID_W = 64
POOL_WINDOWS = (2, 4, 8, 16)
POOL_GROUP = 128
POOL_WIDTH = POOL_GROUP * len(POOL_WINDOWS)
N_HEADS = 8
QK_NOPE = 64
QK_ROPE = 32
QK_HEAD = QK_NOPE + QK_ROPE
V_HEAD = 64
Q_RANK = 384
KV_RANK = 256
ATTN_WIDTH = N_HEADS * V_HEAD
N_EXPERTS = 16
N_GROUPS = 4
EXPERTS_PER_GROUP = N_EXPERTS // N_GROUPS
D_EXPERT = 256
ROPE_BASE = 10000.0
EPS = 1e-6

LANE = 128
HEAD_PAD = N_HEADS * LANE
ONES_ROWS = 16
V_ROWS = V_HEAD + ONES_ROWS
VT_ROWS = N_HEADS * V_ROWS
POOL_HALO = max(POOL_WINDOWS)
TOKEN_BLOCK = 256
EXP_LAG = 3
PV_LAG = 2
PROJ_BLOCK = 1024
N_SUB = 2
MOE_BLOCK = 1024
EXPERT_CHUNK = 4
VMEM_LIMIT = 56 * 1024 * 1024

BF16 = jnp.bfloat16
F32 = jnp.float32


def _dot(a, b):
    return jnp.dot(a, b, preferred_element_type=F32)


def _dot_nt(a, b):
    return lax.dot_general(a, b, (((1,), (1,)), ((), ())), preferred_element_type=F32)


def _rms(x, g):
    return x * lax.rsqrt(jnp.mean(x * x, axis=-1, keepdims=True) + EPS) * g


def _sigmoid(x):
    return 1.0 / (1.0 + jnp.exp(-x))


def _call(body, layer, grid, in_specs, out_specs, out_shape, args, name, scratch_shapes=(), flags=None):
    return pl.pallas_call(
        body,
        grid_spec=pltpu.PrefetchScalarGridSpec(
            num_scalar_prefetch=1, grid=grid, in_specs=in_specs, out_specs=out_specs, scratch_shapes=scratch_shapes),
        out_shape=out_shape,
        compiler_params=pltpu.CompilerParams(
            dimension_semantics=("arbitrary",) * len(grid), vmem_limit_bytes=VMEM_LIMIT, flags=flags),
        name=name,
    )(layer, *args)


def _layer_spec(shape):
    return pl.BlockSpec((None,) + shape, lambda *a: (a[-1][0],) + (0,) * len(shape))


def _cond_spec(latent, blocks_per_seq):
    if latent:
        return pl.BlockSpec((None, None, 6, D_MODEL), lambda j, *a: (a[-1][0], 1 + j // blocks_per_seq, 0, 0))
    return pl.BlockSpec((None, None, 6, D_MODEL), lambda j, *a: (a[-1][0], 0, 0, 0))


def _mod_kernel(cond_ref, w_ref, b_ref, o_ref):
    cond = cond_ref[...]
    s = (cond * _sigmoid(cond)).astype(BF16)
    o_ref[...] = _dot(s, w_ref[...].astype(BF16)) + b_ref[...]


def _mod_table(conds, ada_w, ada_b):
    n_rows = conds.shape[0]
    tn = 1536
    return pl.pallas_call(
        _mod_kernel,
        grid=(DEPTH, 6 * D_MODEL // tn),
        in_specs=[
            pl.BlockSpec((n_rows, D_MODEL), lambda l, n: (0, 0)),
            pl.BlockSpec((None, D_MODEL, tn), lambda l, n: (l, 0, n)),
            pl.BlockSpec((None, 1, tn), lambda l, n: (l, 0, n)),
        ],
        out_specs=pl.BlockSpec((None, n_rows, tn), lambda l, n: (l, 0, n)),
        out_shape=jax.ShapeDtypeStruct((DEPTH, n_rows, 6 * D_MODEL), F32),
        compiler_params=pltpu.CompilerParams(
            dimension_semantics=("arbitrary", "arbitrary"), vmem_limit_bytes=VMEM_LIMIT),
        name="mod_table",
    )(conds, ada_w, ada_b.reshape(DEPTH, 1, 6 * D_MODEL))


def _head_norm(t, g):
    ms = jnp.sum(t * t, axis=-1, keepdims=True) * (1.0 / QK_HEAD)
    return t * lax.rsqrt(ms + EPS) * g


def _value_rows(vt):
    ones = jnp.ones((ONES_ROWS, vt.shape[1]), BF16)
    parts = []
    for h in range(N_HEADS):
        parts += [vt[h * V_HEAD:(h + 1) * V_HEAD, :].astype(BF16), ones]
    return jnp.concatenate(parts, axis=0)


def _cache_keys_kernel(ckv_ref, kr_ref, wuk_ref, wuvt_ref, gk_ref, k_ref, vt_ref):
    ckv = ckv_ref[...].astype(BF16)
    kn = _dot(ckv, wuk_ref[...])
    vt_ref[...] = _value_rows(_dot_nt(wuvt_ref[...], ckv))
    kr = kr_ref[...]
    for h in range(N_HEADS):
        sl = slice(h * LANE, (h + 1) * LANE)
        k_ref[:, sl] = _head_norm(kn[:, sl] + kr, gk_ref[...]).astype(BF16)


def _cache_keys(cache_ckv, cache_kr_pad, wuk, wuvt, gk):
    nb, _, n, _ = cache_ckv.shape
    return pl.pallas_call(
        _cache_keys_kernel,
        grid=(DEPTH, nb),
        in_specs=[
            pl.BlockSpec((None, None, n, KV_RANK), lambda l, b: (b, l, 0, 0)),
            pl.BlockSpec((None, None, n, LANE), lambda l, b: (b, l, 0, 0)),
            pl.BlockSpec((None, KV_RANK, HEAD_PAD), lambda l, b: (l, 0, 0)),
            pl.BlockSpec((None, ATTN_WIDTH, KV_RANK), lambda l, b: (l, 0, 0)),
            pl.BlockSpec((None, 1, LANE), lambda l, b: (l, 0, 0)),
        ],
        out_specs=[
            pl.BlockSpec((None, None, n, HEAD_PAD), lambda l, b: (l, b, 0, 0)),
            pl.BlockSpec((None, None, VT_ROWS, n), lambda l, b: (l, b, 0, 0)),
        ],
        out_shape=[jax.ShapeDtypeStruct((DEPTH, nb, n, HEAD_PAD), BF16),
                   jax.ShapeDtypeStruct((DEPTH, nb, VT_ROWS, n), BF16)],
        compiler_params=pltpu.CompilerParams(
            dimension_semantics=("arbitrary", "arbitrary"), vmem_limit_bytes=VMEM_LIMIT),
        name="cache_keys",
    )(cache_ckv, cache_kr_pad, wuk, wuvt, gk)


def _lanes(col_ref, n):
    return jnp.concatenate([col_ref[...]] * (n // LANE), axis=1)


def _rms_t(t, g):
    return t * lax.rsqrt(jnp.mean(t * t, axis=0, keepdims=True) + EPS) * g


def _head_t(t, g, rope):
    t = t * lax.rsqrt(jnp.mean(t * t, axis=0, keepdims=True) + EPS) * g
    if rope is not None:
        cos, sin = rope
        half = QK_ROPE // 2
        r1 = t[QK_NOPE:QK_NOPE + half, :]
        r2 = t[QK_NOPE + half:, :]
        t = jnp.concatenate([t[:QK_NOPE, :], r1 * cos - r2 * sin, r1 * sin + r2 * cos], axis=0)
    return t


def _inproj_kernel(*refs, latent):
    if latent:
        (_, x_ref, mod_ref, g1_ref, wu_ref, wrt_ref, qag_ref, wuqt_ref, kvg_ref, wukt_ref, wuvt_ref, gq_ref, gk_ref,
         rope_ref, u_ref, q_ref, k_ref, vt_ref) = refs
    else:
        (_, x_ref, mod_ref, g1_ref, wu_ref, wrt_ref, qag_ref, wuqt_ref, kvg_ref, wukt_ref, wuvt_ref, gq_ref, gk_ref,
         u_ref, q_ref, k_ref, vt_ref, ckv_ref, kr_ref) = refs
    n = x_ref.shape[0] // N_SUB
    shift1 = mod_ref[0:1, :]
    scale1 = mod_ref[1:2, :]
    gain1 = g1_ref[...] * (1.0 + scale1)
    zeros = jnp.zeros((LANE - QK_HEAD, n), F32)

    def modulate(rows):
        return (_rms(x_ref[rows, :], gain1) + shift1).astype(BF16)

    def project(rows, h):
        u_ref[rows, :] = _dot(h, wu_ref[...])
        return _dot_nt(wrt_ref[...], h)

    def latent_norms(pt):
        cq = _rms_t(pt[:Q_RANK, :], _lanes(qag_ref, n))
        ckv = _rms_t(pt[Q_RANK:Q_RANK + KV_RANK, :], _lanes(kvg_ref, n))
        return cq.astype(BF16), ckv, pt[Q_RANK + KV_RANK:, :]

    def up_project(rows, cq_b, ckv):
        ckv_b = ckv.astype(BF16)
        qt = _dot(wuqt_ref[...], cq_b)
        knt = _dot(wukt_ref[...], ckv_b)
        vt_ref[:, rows] = _value_rows(_dot(wuvt_ref[...], ckv_b))
        return qt, knt

    def heads(rows, qt, knt, ckv, krt):
        rope = (rope_ref[0:QK_ROPE // 2, rows], rope_ref[QK_ROPE // 2:QK_ROPE, rows]) if latent else None
        gq = _lanes(gq_ref, n)
        gk = _lanes(gk_ref, n)
        k_heads = []
        for hd in range(N_HEADS):
            qh = _head_t(qt[hd * QK_HEAD:(hd + 1) * QK_HEAD, :], gq, rope)
            q_ref[hd * LANE:hd * LANE + QK_HEAD, rows] = qh.astype(BF16)
            q_ref[hd * LANE + QK_HEAD:(hd + 1) * LANE, rows] = zeros.astype(BF16)
            kh = jnp.concatenate([knt[hd * QK_NOPE:(hd + 1) * QK_NOPE, :], krt], axis=0)
            k_heads += [_head_t(kh, gk, rope), zeros]
        k_ref[rows, :] = jnp.concatenate(k_heads, axis=0).T.astype(BF16)
        if not latent:
            ckv_ref[rows, :] = ckv.T
            kr_ref[rows, :] = jnp.concatenate([jnp.zeros((QK_NOPE, n), F32), krt, zeros], axis=0).T

    subs = [slice(i * n, (i + 1) * n) for i in range(N_SUB)]
    hs = [modulate(r) for r in subs]
    pts = [project(r, h) for r, h in zip(subs, hs)]
    normed = [latent_norms(pt) for pt in pts]
    ups = [up_project(r, cq_b, ckv) for r, (cq_b, ckv, _) in zip(subs, normed)]
    for r, (qt, knt), (_, ckv, krt) in zip(subs, ups, normed):
        heads(r, qt, knt, ckv, krt)


def _inproj(layer, x, mod, w, *, latent, seq_len, rope=None):
    t = x.shape[0]
    tb = min(PROJ_BLOCK, seq_len) if latent else min(PROJ_BLOCK, t)
    bps = seq_len // tb
    rest = Q_RANK + KV_RANK + QK_ROPE
    in_specs = [
        pl.BlockSpec((tb, D_MODEL), lambda j, l: (j, 0)),
        _cond_spec(latent, bps),
        _layer_spec((1, D_MODEL)),
        _layer_spec((D_MODEL, POOL_WIDTH)),
        _layer_spec((rest, D_MODEL)),
        _layer_spec((Q_RANK, LANE)),
        _layer_spec((N_HEADS * QK_HEAD, Q_RANK)),
        _layer_spec((KV_RANK, LANE)),
        _layer_spec((N_HEADS * QK_NOPE, KV_RANK)),
        _layer_spec((ATTN_WIDTH, KV_RANK)),
        _layer_spec((QK_HEAD, LANE)),
        _layer_spec((QK_HEAD, LANE)),
    ]
    args = [x, mod, w['g1'], w['w_u'], w['w_rest_t'], w['qag_col'], w['w_uq_t'], w['kvg_col'], w['w_uk_t'],
            w['w_uvt'], w['gq_col'], w['gk_col']]
    tok = lambda width: pl.BlockSpec((tb, width), lambda j, l: (j, 0))
    vt_spec = pl.BlockSpec((VT_ROWS, tb), lambda j, l: (0, j))
    out_specs = [tok(POOL_WIDTH), pl.BlockSpec((HEAD_PAD, tb), lambda j, l: (0, j)), tok(HEAD_PAD), vt_spec]
    out_shape = [jax.ShapeDtypeStruct((t, POOL_WIDTH), F32), jax.ShapeDtypeStruct((HEAD_PAD, t), BF16),
                 jax.ShapeDtypeStruct((t, HEAD_PAD), BF16), jax.ShapeDtypeStruct((VT_ROWS, t), BF16)]
    if latent:
        in_specs.append(pl.BlockSpec((QK_ROPE, tb), lambda j, l: (0, j % bps)))
        args.append(rope)
    else:
        out_specs += [tok(KV_RANK), tok(LANE)]
        out_shape += [jax.ShapeDtypeStruct((t, KV_RANK), F32), jax.ShapeDtypeStruct((t, LANE), F32)]
    return _call(functools.partial(_inproj_kernel, latent=latent), layer, (t // tb,), in_specs, out_specs, out_shape,
                 args, "inproj_latent" if latent else "inproj_context")


def _attn_kernel(*refs, n_seg):
    q_ref = refs[1]
    o_ref = refs[-1]
    tb = TOKEN_BLOCK
    tiles = [(s, c) for s in range(n_seg) for c in range(refs[2 + 2 * s].shape[0] // tb)]

    def fold(x):
        return x.reshape(x.shape[0] // 8, 8, x.shape[1])

    def score_tile(hd, s, c):
        sl = slice(hd * LANE, (hd + 1) * LANE)
        st = _dot(refs[2 + 2 * s][c * tb:(c + 1) * tb, sl], q_ref[sl, :])
        return st, jnp.max(fold(st), axis=0)

    def running_max(m8, mt):
        return mt if m8 is None else jnp.maximum(m8, mt)

    st_cur, m8 = [], None
    for s, c in tiles:
        st, mt = score_tile(0, s, c)
        st_cur.append(st)
        m8 = running_max(m8, mt)
    outs = []
    for hd in range(N_HEADS):
        vs = slice(hd * V_ROWS, (hd + 1) * V_ROWS)
        m = jnp.max(m8, axis=0, keepdims=True)
        st_next, m8 = [], None
        acc = None
        pending = []
        probes = []

        def pv_step(acc, item):
            pb, s, c = item
            pv = _dot(refs[3 + 2 * s][vs, c * tb:(c + 1) * tb], pb)
            return pv if acc is None else acc + pv

        for i, (s, c) in enumerate(tiles):
            m_i = m
            if hd + 1 < N_HEADS:
                st, mt = score_tile(hd + 1, s, c)
                st_next.append(st)
                m8 = running_max(m8, mt)
                probes.append(mt[0:1, :])
                if i >= EXP_LAG:
                    probe = probes[i - EXP_LAG]
                    m_i = jnp.where(probe != probe, probe, m)
            if len(pending) >= PV_LAG:
                acc = pv_step(acc, pending.pop(0))
            pending.append((jnp.exp2((st_cur[i] - m_i).astype(BF16)), s, c))
        for item in pending:
            acc = pv_step(acc, item)
        outs.append(acc[:V_HEAD, :] * (1.0 / acc[V_HEAD:V_HEAD + 1, :]))
        st_cur = st_next
    o_ref[...] = jnp.concatenate(outs, axis=0).T.astype(BF16)


def _attention(layer, q, segs, *, seq_len):
    t = q.shape[1]
    tb = TOKEN_BLOCK
    bps = seq_len // tb
    in_specs = [pl.BlockSpec((HEAD_PAD, tb), lambda j, l: (0, j))]
    args = [q]
    for k, vt, per_layer in segs:
        n = k.shape[-2]
        if per_layer:
            in_specs += [pl.BlockSpec((None, None, n, HEAD_PAD), lambda j, l: (l[0], j // bps, 0, 0)),
                         pl.BlockSpec((None, None, VT_ROWS, n), lambda j, l: (l[0], j // bps, 0, 0))]
        else:
            in_specs += [pl.BlockSpec((None, n, HEAD_PAD), lambda j, l: (j // bps, 0, 0)),
                         pl.BlockSpec((VT_ROWS, n), lambda j, l: (0, j // bps))]
        args += [k, vt]
    return _call(functools.partial(_attn_kernel, n_seg=len(segs)), layer, (t // tb,), in_specs,
                 pl.BlockSpec((tb, ATTN_WIDTH), lambda j, l: (j, 0)), jax.ShapeDtypeStruct((t, ATTN_WIDTH), BF16),
                 args, "attention_%dseg" % len(segs))


def _route(logits_t, bias):
    scores = _sigmoid(logits_t)
    sel = scores + bias
    s_rows = [scores[e:e + 1, :] for e in range(N_EXPERTS)]
    r = [sel[e:e + 1, :] for e in range(N_EXPERTS)]
    picked = []
    group_score = []
    for g in range(N_GROUPS):
        members = range(g * EXPERTS_PER_GROUP, (g + 1) * EXPERTS_PER_GROUP)
        total = None
        for e in members:
            rank = None
            for j in members:
                if j == e:
                    continue
                ahead = (r[j] >= r[e]) if j < e else (r[j] > r[e])
                ahead = ahead.astype(F32)
                rank = ahead if rank is None else rank + ahead
            pick = rank < 2.0
            picked.append(pick)
            contrib = jnp.where(pick, r[e], 0.0)
            total = contrib if total is None else total + contrib
        group_score.append(total)
    gates = []
    for g in range(N_GROUPS):
        best = None
        for j in range(N_GROUPS):
            if j == g:
                continue
            wins = (group_score[g] > group_score[j]) if j < g else (group_score[g] >= group_score[j])
            best = wins if best is None else jnp.logical_and(best, wins)
        for e in range(g * EXPERTS_PER_GROUP, (g + 1) * EXPERTS_PER_GROUP):
            gates.append(jnp.where(jnp.logical_and(best, picked[e]), s_rows[e], 0.0))
    denom = functools.reduce(jnp.add, gates)
    inv = 1.0 / denom
    return [g * inv for g in gates]


def _post_kernel(_, x_ref, up_ref, uc_ref, un_ref, attn_ref, mod_ref, poolw_ref, pscale_ref, wout_ref, g2_ref,
                 rwt_ref, rb_ref, x1_ref, h2_ref, gates_ref, uext_ref, *, blocks_per_seq, seq_len):
    tb = x_ref.shape[0]
    halo = POOL_HALO
    jb = pl.program_id(0) % blocks_per_seq
    uext_ref[0:halo, :] = jnp.where(jb == 0, 0.0, up_ref[tb - halo:tb, :])
    uext_ref[halo:halo + tb, :] = uc_ref[...]
    uext_ref[halo + tb:, :] = jnp.where(jb == blocks_per_seq - 1, 0.0, un_ref[0:halo, :])
    n = tb // N_SUB
    gate1 = mod_ref[2:3, :]
    shift2 = mod_ref[3:4, :]
    gain2 = g2_ref[...] * (1.0 + mod_ref[4:5, :])

    def pool_diffs(r0):
        pos = jb * tb + r0 + lax.broadcasted_iota(jnp.int32, (n, 1), 0)
        diffs = []
        for g, w in enumerate(POOL_WINDOWS):
            half = w // 2
            cols = slice(g * POOL_GROUP, (g + 1) * POOL_GROUP)
            total = None
            for d in range(-half, half):
                part = uext_ref[halo + r0 + d:halo + r0 + d + n, cols]
                total = part if total is None else total + part
            count = jnp.minimum(pos + half, seq_len) - jnp.maximum(pos - half, 0)
            diffs.append((total * (1.0 / count.astype(F32)) - uc_ref[r0:r0 + n, cols]).astype(BF16))
        return diffs

    def mix(r0, diffs):
        pooled = [_dot(d, poolw_ref[g]) for g, d in enumerate(diffs)]
        pool = jnp.concatenate(pooled, axis=-1) * pscale_ref[...]
        return _dot(jnp.concatenate([pool.astype(BF16), attn_ref[r0:r0 + n, :]], axis=-1), wout_ref[...])

    def residual_norm(r0, mixed):
        x1 = x_ref[r0:r0 + n, :] + gate1 * mixed
        x1_ref[r0:r0 + n, :] = x1
        h2 = (_rms(x1, gain2) + shift2).astype(BF16)
        h2_ref[r0:r0 + n, :] = h2
        return h2

    def gates(r0, logits_t):
        gate_rows = _route(logits_t, rb_ref[...])
        zeros = jnp.zeros((LANE - EXPERT_CHUNK, n), F32)
        pieces = []
        for c in range(N_EXPERTS // EXPERT_CHUNK):
            pieces += gate_rows[c * EXPERT_CHUNK:(c + 1) * EXPERT_CHUNK] + [zeros]
        gates_ref[r0:r0 + n, :] = jnp.concatenate(pieces, axis=0).T

    starts = [i * n for i in range(N_SUB)]
    diffs = [pool_diffs(r0) for r0 in starts]
    mixed = [mix(r0, d) for r0, d in zip(starts, diffs)]
    h2s = [residual_norm(r0, m) for r0, m in zip(starts, mixed)]
    logits = [_dot_nt(rwt_ref[...], h2) for h2 in h2s]
    for r0, lt in zip(starts, logits):
        gates(r0, lt)


def _post(layer, x, u, attn, mod, w, router_wt, router_b, *, latent, seq_len):
    t = x.shape[0]
    tb = min(PROJ_BLOCK, seq_len)
    bps = seq_len // tb
    n_chunks = N_EXPERTS // EXPERT_CHUNK
    n_blocks = t // tb
    tok = lambda width: pl.BlockSpec((tb, width), lambda j, l: (j, 0))
    in_specs = [
        tok(D_MODEL),
        pl.BlockSpec((tb, POOL_WIDTH), lambda j, l: (jnp.maximum(j - 1, 0), 0)),
        tok(POOL_WIDTH),
        pl.BlockSpec((tb, POOL_WIDTH), lambda j, l: (jnp.minimum(j + 1, n_blocks - 1), 0)),
        tok(ATTN_WIDTH),
        _cond_spec(latent, bps),
        _layer_spec((len(POOL_WINDOWS), POOL_GROUP, POOL_GROUP)),
        _layer_spec((1, POOL_WIDTH)),
        _layer_spec((2 * ATTN_WIDTH, D_MODEL)),
        _layer_spec((1, D_MODEL)),
        pl.BlockSpec((N_EXPERTS, D_MODEL), lambda j, l: (0, 0)),
        pl.BlockSpec((N_EXPERTS, 1), lambda j, l: (0, 0)),
    ]
    out_shape = [
        jax.ShapeDtypeStruct((t, D_MODEL), F32),
        jax.ShapeDtypeStruct((t, D_MODEL), BF16),
        jax.ShapeDtypeStruct((t, n_chunks * LANE), F32),
    ]
    return _call(functools.partial(_post_kernel, blocks_per_seq=bps, seq_len=seq_len), layer, (n_blocks,), in_specs,
                 [tok(D_MODEL), tok(D_MODEL), tok(n_chunks * LANE)], out_shape,
                 [x, u, u, u, attn, mod, w['pool_w'], w['pool_scale'], w['w_out'], w['g2'], router_wt, router_b],
                 "post_latent" if latent else "post_context",
                 scratch_shapes=[pltpu.VMEM((tb + 2 * POOL_HALO, POOL_WIDTH), F32)])


def _moe_kernel(_, h2_ref, gates_ref, x1_ref, mod_ref, wg_ref, wu_ref, wd_ref, o_ref, acc_ref):
    c = pl.program_id(1)

    @pl.when(c == 0)
    def _():
        acc_ref[...] = jnp.zeros_like(acc_ref)

    h = h2_ref[...]

    def gate_up(e):
        return _dot(h, wg_ref[e]), _dot(h, wu_ref[e])

    total = None
    ab = gate_up(0)
    for e in range(EXPERT_CHUNK):
        ab_next = gate_up(e + 1) if e + 1 < EXPERT_CHUNK else None
        a, b = ab
        act = a * _sigmoid(a) * b * gates_ref[:, e:e + 1]
        y = _dot(act.astype(BF16), wd_ref[e])
        total = y if total is None else total + y
        ab = ab_next
    acc_ref[...] += total

    @pl.when(c == pl.num_programs(1) - 1)
    def _():
        o_ref[...] = x1_ref[...] + mod_ref[5:6, :] * acc_ref[...]


def _moe(layer, h2, gates, x1, mod, w, *, latent, seq_len):
    t = h2.shape[0]
    tm = MOE_BLOCK
    bps = seq_len // tm if latent else 1
    if latent:
        cond = pl.BlockSpec((None, None, 6, D_MODEL), lambda i, c, l: (l[0], 1 + i // bps, 0, 0))
    else:
        cond = pl.BlockSpec((None, None, 6, D_MODEL), lambda i, c, l: (l[0], 0, 0, 0))
    in_specs = [
        pl.BlockSpec((tm, D_MODEL), lambda i, c, l: (i, 0)),
        pl.BlockSpec((tm, LANE), lambda i, c, l: (i, c)),
        pl.BlockSpec((tm, D_MODEL), lambda i, c, l: (i, 0)),
        cond,
        pl.BlockSpec((None, EXPERT_CHUNK, D_MODEL, D_EXPERT), lambda i, c, l: (l[0], c, 0, 0)),
        pl.BlockSpec((None, EXPERT_CHUNK, D_MODEL, D_EXPERT), lambda i, c, l: (l[0], c, 0, 0)),
        pl.BlockSpec((None, EXPERT_CHUNK, D_EXPERT, D_MODEL), lambda i, c, l: (l[0], c, 0, 0)),
    ]
    return _call(_moe_kernel, layer, (t // tm, N_EXPERTS // EXPERT_CHUNK), in_specs,
                 pl.BlockSpec((tm, D_MODEL), lambda i, c, l: (i, 0)), jax.ShapeDtypeStruct((t, D_MODEL), F32),
                 [h2, gates, x1, mod, w['w_gate'], w['w_up'], w['w_down']],
                 "experts_latent" if latent else "experts_context",
                 scratch_shapes=[pltpu.VMEM((tm, D_MODEL), F32)])


def _pad_heads(w, width):
    lead = w.shape[:-1]
    w = w.reshape(lead + (N_HEADS, width))
    w = jnp.pad(w, [(0, 0)] * len(lead) + [(0, 0), (0, LANE - width)])
    return w.reshape(lead + (HEAD_PAD,))


def _pad_gain(g):
    return jnp.pad(g, ((0, 0), (0, LANE - QK_HEAD)))[:, None, :]


def _rope_table(rows):
    row = jnp.repeat(jnp.arange(rows), GRID_W).astype(F32)
    col = jnp.tile(jnp.arange(GRID_W), rows).astype(F32)
    n_freq = QK_ROPE // 4
    inv = ROPE_BASE ** (-jnp.arange(n_freq, dtype=F32) / n_freq)
    ang = jnp.concatenate([row[:, None] * inv, col[:, None] * inv], axis=-1)
    return jnp.concatenate([jnp.cos(ang), jnp.sin(ang)], axis=-1).T


def _col(g, rows):
    g = jnp.pad(g, ((0, 0), (0, rows - g.shape[1])))
    return jnp.broadcast_to(g[:, :, None], g.shape + (LANE,))


def kernel(x_prompt, x_sample, cache_ckv, cache_krope, c, c_ctx, ada_w, ada_b, norm1_g, norm2_g, w_in, pool_w,
           pool_scale, q_a_norm_g, w_uq, kv_a_norm_g, w_ukv, q_norm_g, k_norm_g, w_out, router_w, router_b,
           w_gate, w_up, w_down):
    batch, seq, _ = x_prompt.shape
    dec_batch, dec_seq, _ = x_sample.shape

    conds = jnp.concatenate([c_ctx[None, :], c, jnp.zeros((8 - 1 - dec_batch, D_MODEL), F32)], axis=0)
    mod = _mod_table(conds, ada_w, ada_b).reshape(DEPTH, 8, 6, D_MODEL)

    w_ukv_h = w_ukv.reshape(DEPTH, KV_RANK, N_HEADS, QK_NOPE + V_HEAD)
    w_uk = w_ukv_h[..., :QK_NOPE].reshape(DEPTH, KV_RANK, N_HEADS * QK_NOPE)
    q_scale = math.log2(math.e) / math.sqrt(QK_HEAD)
    weights = {
        'g1': norm1_g[:, None, :],
        'g2': norm2_g[:, None, :],
        'w_u': w_in[:, :, :POOL_WIDTH].astype(BF16),
        'w_rest_t': jnp.swapaxes(w_in[:, :, POOL_WIDTH:], 1, 2).astype(BF16),
        'qag_col': _col(q_a_norm_g, Q_RANK),
        'kvg_col': _col(kv_a_norm_g, KV_RANK),
        'w_uq_t': jnp.swapaxes(w_uq, 1, 2).astype(BF16),
        'w_uk_t': jnp.swapaxes(w_uk, 1, 2).astype(BF16),
        'w_uk': _pad_heads(w_uk, QK_NOPE).astype(BF16),
        'w_uvt': jnp.swapaxes(w_ukv_h[..., QK_NOPE:].reshape(DEPTH, KV_RANK, ATTN_WIDTH), 1, 2).astype(BF16),
        'gq_col': _col(q_norm_g * q_scale, QK_HEAD),
        'gk_col': _col(k_norm_g, QK_HEAD),
        'gk': _pad_gain(k_norm_g),
        'pool_w': pool_w.astype(BF16),
        'pool_scale': pool_scale[:, None, :],
        'w_out': w_out.astype(BF16),
        'w_gate': w_gate.astype(BF16),
        'w_up': w_up.astype(BF16),
        'w_down': w_down.astype(BF16),
    }
    router_wt = router_w.T.astype(BF16)
    router_bc = router_b[:, None]
    rope = _rope_table(dec_seq // GRID_W)

    cache_kr_pad = jnp.pad(cache_krope, ((0, 0), (0, 0), (0, 0), (QK_NOPE, LANE - QK_HEAD)))
    k_cache, vt_cache = _cache_keys(cache_ckv, cache_kr_pad, weights['w_uk'], weights['w_uvt'], weights['gk'])

    def layer_step(carry, layer):
        xc, xl = carry
        u, q, k, vt, ckv, kr = _inproj(layer, xc, mod, weights, latent=False, seq_len=seq)
        ctx_seg = (k.reshape(batch, seq, HEAD_PAD), vt, False)
        attn = _attention(layer, q, [ctx_seg], seq_len=seq)
        x1, h2, gates = _post(layer, xc, u, attn, mod, weights, router_wt, router_bc, latent=False, seq_len=seq)
        xc = _moe(layer, h2, gates, x1, mod, weights, latent=False, seq_len=seq)
        u, q, k, vt = _inproj(layer, xl, mod, weights, latent=True, seq_len=dec_seq, rope=rope)
        lat_seg = (k.reshape(dec_batch, dec_seq, HEAD_PAD), vt, False)
        attn = _attention(layer, q, [(k_cache, vt_cache, True), lat_seg], seq_len=dec_seq)
        x1, h2, gates = _post(layer, xl, u, attn, mod, weights, router_wt, router_bc, latent=True, seq_len=dec_seq)
        xl = _moe(layer, h2, gates, x1, mod, weights, latent=True, seq_len=dec_seq)
        return (xc, xl), (ckv, kr)

    layers = jnp.arange(DEPTH, dtype=jnp.int32).reshape(DEPTH, 1)
    (xc, xl), (ckv_all, kr_all) = lax.scan(
        layer_step, (x_prompt.reshape(batch * seq, D_MODEL), x_sample.reshape(dec_batch * dec_seq, D_MODEL)), layers,
        unroll=True)

    new_ckv = jnp.swapaxes(ckv_all.reshape(DEPTH, batch, seq, KV_RANK), 0, 1)
    new_krope = jnp.swapaxes(kr_all[:, :, QK_NOPE:QK_HEAD].reshape(DEPTH, batch, seq, QK_ROPE), 0, 1)
    return (xc.reshape(batch, seq, D_MODEL), xl.reshape(dec_batch, dec_seq, D_MODEL), new_ckv, new_krope)
```
